```python
import math
import jax, jax.numpy as jnp
from jax import lax
import numpy as np

D_MODEL = 1024
BATCH = 4
SEQ = 4096
DEPTH = 4
DEC_BATCH = 32
DEC_SEQ = 8
PAST_LEN = 8192
PAGE_SIZE = 128

N_EVEN = (DEPTH + 1) // 2
N_ODD = DEPTH // 2
BLOCK_Q = 128
H_A = 4
DK_A = 64
DV_A = 2 * DK_A
H_B = 8
D_B = 64
H_C = 16
D_C = 64
N_MEM = 256
H_M = 4
D_M = D_MODEL // H_M
D_FF = ((8 * D_MODEL // 3) + 127) // 128 * 128
N_EXPERTS = 8
TOP_K = 2
ROPE_THETA = 10000.0
LN_EPS = 1e-5
RMS_EPS = 1e-5
ALPHA = (2 * DEPTH) ** 0.25
BETA_INIT = (8 * DEPTH) ** -0.25
QA = H_A * 2 * DK_A
VA = H_A * DV_A
QB = H_B * D_B
EVEN_SPLITS = (QA, 2 * QA, 2 * QA + VA, 2 * QA + VA + QB, 2 * QA + VA + 2 * QB)
EVEN_IN = 2 * QA + VA + 3 * QB
EVEN_OUT = VA + QB
QC = H_C * D_C
ODD_SPLITS = (QC, 2 * QC, 3 * QC)
ODD_IN = 3 * QC + H_C
ODD_OUT = QC

kernel_name = 'hybrid_diff_stick_fox_decoder_step'


def layer_norm(x, g, b):
    xf = x.astype(jnp.float32)
    mu = jnp.mean(xf, axis=-1, keepdims=True)
    var = jnp.mean(jnp.square(xf - mu), axis=-1, keepdims=True)
    return ((xf - mu) * lax.rsqrt(var + LN_EPS) * g.astype(jnp.float32) + b.astype(jnp.float32)).astype(x.dtype)


def rms_norm(x, g):
    xf = x.astype(jnp.float32)
    return (xf * lax.rsqrt(jnp.mean(xf * xf, axis=-1, keepdims=True) + RMS_EPS) * g.astype(jnp.float32)).astype(x.dtype)


def rope(x, pos):
    half = x.shape[-1] // 2
    inv = 1.0 / (ROPE_THETA ** (jnp.arange(half, dtype=jnp.float32) / half))
    ang = pos.astype(jnp.float32)[:, None] * inv[None, :]
    cos = jnp.cos(ang)[:, None, :]
    sin = jnp.sin(ang)[:, None, :]
    xf = x.astype(jnp.float32)
    x1, x2 = xf[..., :half], xf[..., half:]
    return jnp.concatenate([x1 * cos - x2 * sin, x2 * cos + x1 * sin], axis=-1).astype(x.dtype)


def sweep_query_blocks(fn, q_pos, *q_args):
    L = q_pos.shape[0]
    if L <= BLOCK_Q:
        return fn(q_pos, *q_args)
    nb = -(-L // BLOCK_Q)
    pad = nb * BLOCK_Q - L
    if pad:
        q_pos = jnp.concatenate([q_pos, jnp.full((pad,), q_pos[-1], q_pos.dtype)])
        q_args = tuple(jnp.pad(a, [(0, 0), (0, pad)] + [(0, 0)] * (a.ndim - 2)) for a in q_args)

    def to_blocks(a):
        return jnp.moveaxis(a.reshape(a.shape[0], nb, BLOCK_Q, *a.shape[2:]), 1, 0)

    xs = (q_pos.reshape(nb, BLOCK_Q),) + tuple(to_blocks(a) for a in q_args)
    out = lax.map(lambda blk: fn(*blk), xs)
    out = jnp.moveaxis(out, 0, 1)
    out = out.reshape(out.shape[0], nb * BLOCK_Q, *out.shape[3:])
    return out[:, :L]


def diff_attn_block(q_pos, q, k, v, k_pos, lam):
    s = jnp.einsum('bqhcd,bkhcd->bhcqk', q, k, preferred_element_type=jnp.float32) * (DK_A ** -0.5)
    s = jnp.where(k_pos[None, :] <= q_pos[:, None], s, -jnp.inf)
    p = jax.nn.softmax(s, axis=-1)
    w = p[:, :, 0] - lam * p[:, :, 1]
    return jnp.einsum('bhqk,bkhe->bqhe', w.astype(v.dtype), v)


def stick_breaking_block(q_pos, q, k, v, k_pos):
    z = jnp.einsum('bqhd,bkhd->bhqk', q, k, preferred_element_type=jnp.float32) * (D_B ** -0.5)
    mask = k_pos[None, :] < q_pos[:, None]
    log_one_minus = jnp.where(mask, jax.nn.log_sigmoid(-z), 0.0)
    last_axis = log_one_minus.ndim - 1
    suffix = lax.cumsum(log_one_minus, axis=last_axis, reverse=True) - log_one_minus
    a = jnp.where(mask, jnp.exp(jax.nn.log_sigmoid(z) + suffix), 0.0)
    return jnp.einsum('bhqk,bkhd->bqhd', a.astype(v.dtype), v)


def fox_block(q_pos, q, f_q, k, v, f_k, k_pos):
    s = jnp.einsum('bqhd,bkhd->bhqk', q, k, preferred_element_type=jnp.float32) * (D_C ** -0.5)
    s = s + (jnp.swapaxes(f_q, 1, 2)[..., :, None] - jnp.swapaxes(f_k, 1, 2)[..., None, :])
    s = jnp.where(k_pos[None, :] <= q_pos[:, None], s, -jnp.inf)
    p = jax.nn.softmax(s, axis=-1)
    return jnp.einsum('bhqk,bkhd->bqhd', p.astype(v.dtype), v)


def even_mixer(x, pos, a_kv_past, b_kv_past, w_in, w_out, lam_vecs, subln_g, lam_init):
    nb, L, _ = x.shape
    h = x @ w_in
    qa, ka, va, qb, kb, vb = jnp.split(h, EVEN_SPLITS, axis=-1)
    qa = rope(qa.reshape(nb, L, 2 * H_A, DK_A), pos).reshape(nb, L, H_A, 2, DK_A)
    ka = rope(ka.reshape(nb, L, 2 * H_A, DK_A), pos)
    a_kv_new = jnp.stack([ka.reshape(nb, L, H_A, 2 * DK_A), va.reshape(nb, L, H_A, DV_A)], axis=2)
    b_kv_new = jnp.stack([kb.reshape(nb, L, H_B, D_B), vb.reshape(nb, L, H_B, D_B)], axis=2)
    a_kv = a_kv_new if a_kv_past is None else jnp.concatenate([a_kv_past, a_kv_new], axis=1)
    b_kv = b_kv_new if b_kv_past is None else jnp.concatenate([b_kv_past, b_kv_new], axis=1)
    lk = a_kv.shape[1]
    k_pos = jnp.arange(lk)
    k_a = a_kv[:, :, 0].reshape(nb, lk, H_A, 2, DK_A)
    v_a = a_kv[:, :, 1]
    lv = lam_vecs.astype(jnp.float32)
    lam = jnp.exp(jnp.sum(lv[0] * lv[1])) - jnp.exp(jnp.sum(lv[2] * lv[3])) + lam_init
    o_a = sweep_query_blocks(lambda qp, q: diff_attn_block(qp, q, k_a, v_a, k_pos, lam), pos, qa)
    o_a = rms_norm(o_a, subln_g) * (1.0 - lam_init)
    k_b, v_b = b_kv[:, :, 0], b_kv[:, :, 1]
    o_b = sweep_query_blocks(lambda qp, q: stick_breaking_block(qp, q, k_b, v_b, k_pos),
                             pos, qb.reshape(nb, L, H_B, D_B))
    o = jnp.concatenate([o_a.reshape(nb, L, VA), o_b.reshape(nb, L, QB)], axis=-1)
    return o @ w_out, a_kv_new, b_kv_new


def odd_mixer(x, pos, c_kv_past, c_logf_past, w_in, b_f, w_out):
    nb, L, _ = x.shape
    h = x @ w_in
    q, k, v, f_logit = jnp.split(h, ODD_SPLITS, axis=-1)
    logf = jax.nn.log_sigmoid((f_logit + b_f).astype(jnp.float32))
    c_kv_new = jnp.stack([k.reshape(nb, L, H_C, D_C), v.reshape(nb, L, H_C, D_C)], axis=2)
    if c_kv_past is None:
        c_kv, logf_all = c_kv_new, logf
    else:
        c_kv = jnp.concatenate([c_kv_past, c_kv_new], axis=1)
        logf_all = jnp.concatenate([c_logf_past.astype(jnp.float32), logf], axis=1)
    lk = c_kv.shape[1]
    k_pos = jnp.arange(lk)
    F = jnp.cumsum(logf_all, axis=1)
    f_q = F[:, lk - L:]
    k_c, v_c = c_kv[:, :, 0], c_kv[:, :, 1]
    o = sweep_query_blocks(lambda qp, qq, fq: fox_block(qp, qq, fq, k_c, v_c, F, k_pos),
                           pos, q.reshape(nb, L, H_C, D_C), f_q)
    return o.reshape(nb, L, ODD_OUT) @ w_out, c_kv_new, logf.astype(x.dtype)


def memory_attn(x, mem_kv, w_q, w_o):
    nb, L, _ = x.shape
    q = (x @ w_q).reshape(nb, L, H_M, D_M)
    s = jnp.einsum('bqhd,bmhd->bhqm', q, mem_kv[:, :, 0], preferred_element_type=jnp.float32) * (D_M ** -0.5)
    p = jax.nn.softmax(s, axis=-1)
    o = jnp.einsum('bhqm,bmhd->bqhd', p.astype(x.dtype), mem_kv[:, :, 1])
    return o.reshape(nb, L, H_M * D_M) @ w_o


def swiglu(x, w_gu, w_down):
    g, u = jnp.split(x @ w_gu, 2, axis=-1)
    return (jax.nn.silu(g) * u) @ w_down


def moe_swiglu(x, w_router, w_gu, w_down):
    logits = jnp.einsum('bld,de->ble', x, w_router, preferred_element_type=jnp.float32)
    top_v, top_i = lax.top_k(logits, TOP_K)
    gates = jax.nn.softmax(top_v, axis=-1)
    combine = jnp.einsum('blk,blke->ble', gates, jax.nn.one_hot(top_i, N_EXPERTS, dtype=jnp.float32))
    y = jnp.zeros_like(x)
    for e in range(N_EXPERTS):
        y = y + combine[..., e:e + 1].astype(x.dtype) * swiglu(x, w_gu[e], w_down[e])
    return y


def lambda_init_fn(layer_idx):
    return 0.8 - 0.6 * math.exp(-0.3 * layer_idx)


def setup_inputs(seed: int = 0) -> dict:
    key = jax.random.key(seed)
    ks = iter(jax.random.split(key, 32))
    f32 = jnp.float32

    def nrm(shape, scale):
        return jax.random.normal(next(ks), shape, f32) * scale

    n_pages = PAST_LEN // PAGE_SIZE
    n_used = DEC_BATCH * n_pages
    n_pool = n_used + max(1, n_used // 4)
    x_prompt = nrm((BATCH, SEQ, D_MODEL), 1.0)
    x_sample = nrm((DEC_BATCH, DEC_SEQ, D_MODEL), 1.0)
    cache_a_kv = nrm((N_EVEN, n_pool, PAGE_SIZE, 2, H_A, 2 * DK_A), 1.0)
    cache_b_kv = nrm((N_EVEN, n_pool, PAGE_SIZE, 2, H_B, D_B), 1.0)
    cache_c_kv = nrm((N_ODD, n_pool, PAGE_SIZE, 2, H_C, D_C), 1.0)
    cache_c_logf = jax.nn.log_sigmoid(jnp.linspace(1.0, 6.0, H_C, dtype=f32) + nrm((N_ODD, n_pool, PAGE_SIZE, H_C), 1.0))
    cache_mem_kv = nrm((DEPTH, DEC_BATCH, N_MEM, 2, H_M, D_M), 1.0)
    perm = jax.random.permutation(next(ks), n_pool)
    page_table = perm[:n_used].reshape(DEC_BATCH, n_pages).astype(jnp.int32)
    mem_prompt = nrm((BATCH, N_MEM, D_MODEL), 1.0)
    din = D_MODEL ** -0.5
    return {
        'x_prompt': x_prompt,
        'x_sample': x_sample,
        'cache_a_kv': cache_a_kv,
        'cache_b_kv': cache_b_kv,
        'cache_c_kv': cache_c_kv,
        'cache_c_logf': cache_c_logf,
        'cache_mem_kv': cache_mem_kv,
        'page_table': page_table,
        'mem_prompt': mem_prompt,
        'even_w_in': nrm((N_EVEN, D_MODEL, EVEN_IN), din),
        'even_w_out': nrm((N_EVEN, EVEN_OUT, D_MODEL), EVEN_OUT ** -0.5 * BETA_INIT),
        'diff_lambda': nrm((N_EVEN, 4, DK_A), 0.1),
        'diff_subln_g': 1.0 + nrm((N_EVEN, DV_A), 0.02),
        'odd_w_in': nrm((N_ODD, D_MODEL, ODD_IN), din),
        'odd_b_f': jnp.linspace(1.0, 6.0, H_C, dtype=f32)[None, :] + nrm((N_ODD, H_C), 0.1),
        'odd_w_out': nrm((N_ODD, ODD_OUT, D_MODEL), ODD_OUT ** -0.5 * BETA_INIT),
        'mem_w_q': nrm((DEPTH, D_MODEL, H_M * D_M), din),
        'mem_w_kv': nrm((DEPTH, D_MODEL, 2 * H_M * D_M), din),
        'mem_w_o': nrm((DEPTH, H_M * D_M, D_MODEL), (H_M * D_M) ** -0.5 * BETA_INIT),
        'ffn_w_gu': nrm((N_EVEN, D_MODEL, 2 * D_FF), din),
        'ffn_w_down': nrm((N_EVEN, D_FF, D_MODEL), D_FF ** -0.5 * BETA_INIT),
        'moe_w_router': nrm((N_ODD, D_MODEL, N_EXPERTS), din),
        'moe_w_gu': nrm((N_ODD, N_EXPERTS, D_MODEL, 2 * D_FF), din),
        'moe_w_down': nrm((N_ODD, N_EXPERTS, D_FF, D_MODEL), D_FF ** -0.5 * BETA_INIT),
        'ln_g': 1.0 + nrm((DEPTH, 3, D_MODEL), 0.02),
        'ln_b': nrm((DEPTH, 3, D_MODEL), 0.02),
    }


def reference(x_prompt, x_sample, cache_a_kv, cache_b_kv, cache_c_kv, cache_c_logf, cache_mem_kv,
              page_table, mem_prompt, even_w_in, even_w_out, diff_lambda, diff_subln_g,
              odd_w_in, odd_b_f, odd_w_out, mem_w_q, mem_w_kv, mem_w_o, ffn_w_gu, ffn_w_down,
              moe_w_router, moe_w_gu, moe_w_down, ln_g, ln_b):
    n_pages = page_table.shape[1]
    past_len = n_pages * PAGE_SIZE

    def gather_pages(pool):
        g = pool[page_table]
        return g.reshape(g.shape[0], n_pages * g.shape[2], *g.shape[3:])

    def run(x, sample):
        nb, L, _ = x.shape
        pos = (past_len if sample else 0) + jnp.arange(L)
        a_rows, b_rows, c_rows, lf_rows, mem_rows = [], [], [], [], []
        for l in range(DEPTH):
            i = l // 2
            if sample:
                mem_kv = cache_mem_kv[l]
            else:
                mem_kv = (mem_prompt @ mem_w_kv[l]).reshape(nb, N_MEM, 2, H_M, D_M)
                mem_rows.append(mem_kv)
            if l % 2 == 0:
                a_past = gather_pages(cache_a_kv[i]) if sample else None
                b_past = gather_pages(cache_b_kv[i]) if sample else None
                y, a_new, b_new = even_mixer(x, pos, a_past, b_past, even_w_in[i], even_w_out[i],
                                             diff_lambda[i], diff_subln_g[i], lambda_init_fn(l))
                a_rows.append(a_new)
                b_rows.append(b_new)
            else:
                c_past = gather_pages(cache_c_kv[i]) if sample else None
                lf_past = gather_pages(cache_c_logf[i]) if sample else None
                y, c_new, lf_new = odd_mixer(x, pos, c_past, lf_past, odd_w_in[i], odd_b_f[i], odd_w_out[i])
                c_rows.append(c_new)
                lf_rows.append(lf_new)
            x = layer_norm(ALPHA * x + y, ln_g[l, 0], ln_b[l, 0])
            x = layer_norm(ALPHA * x + memory_attn(x, mem_kv, mem_w_q[l], mem_w_o[l]), ln_g[l, 1], ln_b[l, 1])
            if l % 2 == 0:
                f = swiglu(x, ffn_w_gu[i], ffn_w_down[i])
            else:
                f = moe_swiglu(x, moe_w_router[i], moe_w_gu[i], moe_w_down[i])
            x = layer_norm(ALPHA * x + f, ln_g[l, 2], ln_b[l, 2])
        return x, jnp.stack(a_rows), jnp.stack(b_rows), jnp.stack(c_rows), jnp.stack(lf_rows), mem_rows

    y_prompt, a_kv_prompt, b_kv_prompt, c_kv_prompt, c_logf_prompt, mem_rows = run(x_prompt, False)
    mem_kv_prompt = jnp.stack(mem_rows)
    y_sample, a_kv_sample, b_kv_sample, c_kv_sample, c_logf_sample, _ = run(x_sample, True)
    return (y_prompt, y_sample, a_kv_prompt, a_kv_sample, b_kv_prompt, b_kv_sample,
            c_kv_prompt, c_kv_sample, c_logf_prompt, c_logf_sample, mem_kv_prompt)
```

```python
import functools
import math

import jax
import jax.numpy as jnp
from jax import lax
from jax.experimental import pallas as pl
from jax.experimental.pallas import tpu as pltpu

F32 = jnp.float32
BF16 = jnp.bfloat16

H_A, DK_A = 4, 64
H_B, D_B = 8, 64
H_C, D_C = 16, 64
H_M = 4
N_EXPERTS, TOP_K = 8, 2
ROPE_THETA = 10000.0
LN_EPS = 1e-5
RMS_EPS = 1e-5
PAGE = 128
LANES = 128
NEG = -1e30
VMEM_LIMIT = 56 << 20
PAGES_PER_STEP = 4


def _lambda_init(layer_idx):
    return 0.8 - 0.6 * math.exp(-0.3 * layer_idx)


def _params(*sem):
    return pltpu.CompilerParams(dimension_semantics=sem, vmem_limit_bytes=VMEM_LIMIT)


def _dot(a, b):
    return jnp.dot(a, b, preferred_element_type=F32)


def _dot_nt(a, b):
    return lax.dot_general(a, b, (((1,), (1,)), ((), ())), preferred_element_type=F32)


def _log_sigmoid(z):
    return jnp.minimum(z, 0.0) - jnp.log1p(jnp.exp(-jnp.abs(z)))


def _layer_norm(v, g, b):
    mu = jnp.mean(v, axis=1, keepdims=True)
    d = v - mu
    var = jnp.mean(d * d, axis=1, keepdims=True)
    return d * lax.rsqrt(var + LN_EPS) * g + b


def _split2(x):
    hi = x.astype(BF16)
    lo = (x - hi.astype(F32)).astype(BF16)
    return hi, lo


def _split3(x):
    h1 = x.astype(BF16)
    r1 = x - h1.astype(F32)
    h2 = r1.astype(BF16)
    h3 = (r1 - h2.astype(F32)).astype(BF16)
    return h1, h2, h3


def _iota(shape, dim):
    return lax.broadcasted_iota(jnp.int32, shape, dim)


def _proj_body(*refs, n_rope):
    if n_rope:
        x_ref, w_ref, cos_ref, sin_ref, o_ref = refs
    else:
        x_ref, w_ref, o_ref = refs
    h = _dot(x_ref[...].astype(BF16), w_ref[...])
    if not n_rope:
        o_ref[...] = h
        return
    n = pl.program_id(1)
    tn = h.shape[1]

    @pl.when(n < n_rope)
    def _():
        reps = tn // LANES
        c = jnp.concatenate([cos_ref[...]] * reps, axis=1)
        s = jnp.concatenate([sin_ref[...]] * reps, axis=1)
        lane = _iota(h.shape, 1)
        first = (lane % 64) < 32
        rot = jnp.where(first, pltpu.roll(h, tn - 32, 1), pltpu.roll(h, 32, 1))
        o_ref[...] = h * c + rot * s

    @pl.when(n >= n_rope)
    def _():
        o_ref[...] = h


def _proj(x, w, layer, n_cols, *, tm, tn=512, rope=None, n_rope=0, name):
    M, K = x.shape
    grid = (M // tm, n_cols // tn)
    in_specs = [pl.BlockSpec((tm, K), lambda m, n: (m, 0)),
                pl.BlockSpec((None, K, tn), lambda m, n: (layer, 0, n))]
    args = [x, w]
    if n_rope:
        cos, sin = rope
        nt = cos.shape[0] // tm
        in_specs += [pl.BlockSpec((tm, LANES), lambda m, n: (m % nt, 0)),
                     pl.BlockSpec((tm, LANES), lambda m, n: (m % nt, 0))]
        args += [cos, sin]
    return pl.pallas_call(
        functools.partial(_proj_body, n_rope=n_rope),
        grid=grid, in_specs=in_specs,
        out_specs=pl.BlockSpec((tm, tn), lambda m, n: (m, n)),
        out_shape=jax.ShapeDtypeStruct((M, n_cols), F32),
        compiler_params=_params("parallel", "arbitrary"), name=name)(*args)


def _logf_body(x_ref, w_ref, b_ref, lf_ref, cs_ref, carry_ref, *, with_cumsum):
    t = pl.program_id(1)

    @pl.when(t == 0)
    def _():
        carry_ref[...] = jnp.zeros_like(carry_ref)

    f = _dot(x_ref[...].astype(BF16), w_ref[...]) + b_ref[...]
    lf = _log_sigmoid(f)
    lf_ref[...] = lf
    tl = lf.shape[0]
    if not with_cumsum:
        cs_ref[...] = lf
        return
    tri = (_iota((tl, tl), 1) <= _iota((tl, tl), 0)).astype(BF16)
    parts = _split3(lf)
    cs = carry_ref[...] + _dot(tri, parts[0]) + _dot(tri, parts[1]) + _dot(tri, parts[2])
    cs_ref[...] = cs
    carry_ref[...] = cs[tl - 1:tl, :]


def _logf(x, wf, bf, *, nb, tl, with_cumsum, name):
    M, K = x.shape
    nt = M // nb // tl
    return pl.pallas_call(
        functools.partial(_logf_body, with_cumsum=with_cumsum), grid=(nb, nt),
        in_specs=[pl.BlockSpec((tl, K), lambda b, t: (b * nt + t, 0)),
                  pl.BlockSpec((K, LANES), lambda b, t: (0, 0)),
                  pl.BlockSpec((1, LANES), lambda b, t: (0, 0))],
        out_specs=[pl.BlockSpec((tl, LANES), lambda b, t: (b * nt + t, 0)),
                   pl.BlockSpec((tl, LANES), lambda b, t: (b * nt + t, 0))],
        out_shape=[jax.ShapeDtypeStruct((M, LANES), F32)] * 2,
        scratch_shapes=[pltpu.VMEM((1, LANES), F32)],
        compiler_params=_params("parallel", "arbitrary"), name=name)(x, wf, bf)


def _load_kv(k_ref, v_ref, kb_ref, vb_ref):
    @pl.when(pl.program_id(2) == 0)
    def _():
        kb_ref[...] = k_ref[...].astype(BF16)
        vb_ref[...] = v_ref[...].astype(BF16)


def _softmax_step(s, v, m, l, acc):
    m_new = jnp.maximum(m, jnp.max(s, axis=1, keepdims=True))
    alpha = jnp.exp(m - m_new)
    p = jnp.exp(s - m_new)
    l = alpha * l + jnp.sum(p, axis=1, keepdims=True)
    acc = alpha * acc + _dot(p.astype(BF16), v)
    return m_new, l, acc


def _diff_body(lv_ref, g_ref, q_ref, k_ref, v_ref, o_ref, kb_ref, vb_ref, *, tq, lam_init):
    _load_kv(k_ref, v_ref, kb_ref, vb_ref)
    qi = pl.program_id(2)
    q = q_ref[...] * (DK_A ** -0.5)
    lane = _iota(q.shape, 1)
    qs = (jnp.where(lane < 64, q, 0.0).astype(BF16), jnp.where(lane >= 64, q, 0.0).astype(BF16))
    causal = _iota((tq, tq), 1) <= _iota((tq, tq), 0)

    def step(ki, carry, diag):
        off = pl.multiple_of(ki * tq, tq)
        k = kb_ref[pl.ds(off, tq), :]
        v = vb_ref[pl.ds(off, tq), :]
        out = []
        for c in range(2):
            s = _dot_nt(qs[c], k)
            if diag:
                s = jnp.where(causal, s, NEG)
            out.append(_softmax_step(s, v, *carry[c]))
        return tuple(out)

    z1 = jnp.zeros((tq, 1), F32)
    init = ((z1 + NEG, z1, jnp.zeros((tq, LANES), F32)),) * 2
    carry = lax.fori_loop(0, qi, lambda i, c: step(i, c, False), init)
    (m0, l0, a0), (m1, l1, a1) = step(qi, carry, True)
    lv = lv_ref[...]
    lam = (jnp.exp(jnp.sum(lv[0:1, :] * lv[1:2, :], axis=1, keepdims=True))
           - jnp.exp(jnp.sum(lv[2:3, :] * lv[3:4, :], axis=1, keepdims=True)) + lam_init)
    o = a0 / l0 - lam * (a1 / l1)
    o = o * lax.rsqrt(jnp.mean(o * o, axis=1, keepdims=True) + RMS_EPS) * g_ref[...]
    o_ref[...] = (o * (1.0 - lam_init)).astype(o_ref.dtype)


def _sb_body(q_ref, k_ref, v_ref, o_ref, kb_ref, vb_ref, *, tq):
    _load_kv(k_ref, v_ref, kb_ref, vb_ref)
    qi = pl.program_id(2)
    q = q_ref[...] * (D_B ** -0.5)
    lane = _iota(q.shape, 1)
    rows = _iota((tq, tq), 0)
    cols = _iota((tq, tq), 1)
    later = (rows > cols).astype(BF16)
    strict = cols < rows
    accs = []
    for hh in range(2):
        qh = jnp.where((lane >= 64 * hh) & (lane < 64 * hh + 64), q, 0.0).astype(BF16)

        def step(ki, carry, diag):
            c_run, acc = carry
            off = pl.multiple_of(ki * tq, tq)
            k = kb_ref[pl.ds(off, tq), :]
            v = vb_ref[pl.ds(off, tq), :]
            z = _dot_nt(qh, k)
            ls = _log_sigmoid(z)
            lom = ls - z
            if diag:
                lom = jnp.where(strict, lom, 0.0)
            hi, lo = _split2(lom)
            suffix = _dot(hi, later) + _dot(lo, later)
            a = jnp.exp(ls + suffix + c_run)
            if diag:
                a = jnp.where(strict, a, 0.0)
            acc = acc + _dot(a.astype(BF16), v)
            c_run = c_run + jnp.sum(lom, axis=1, keepdims=True)
            return c_run, acc

        carry = step(qi, (jnp.zeros((tq, 1), F32), jnp.zeros((tq, LANES), F32)), True)
        carry = lax.fori_loop(0, qi, lambda j, c: step(qi - 1 - j, c, False), carry)
        accs.append(carry[1])
    o_ref[...] = jnp.where(lane < 64, accs[0], accs[1]).astype(o_ref.dtype)


def _fox_body(fq_ref, fk_ref, q_ref, k_ref, v_ref, o_ref, kb_ref, vb_ref, *, tq):
    _load_kv(k_ref, v_ref, kb_ref, vb_ref)
    hp = pl.program_id(1)
    qi = pl.program_id(2)
    q = q_ref[...] * (D_C ** -0.5)
    lane = _iota(q.shape, 1)
    causal = _iota((tq, tq), 1) <= _iota((tq, tq), 0)
    fq_all = fq_ref[...]
    outs = []
    for hh in range(2):
        qh = jnp.where((lane >= 64 * hh) & (lane < 64 * hh + 64), q, 0.0).astype(BF16)
        fq = jnp.sum(jnp.where(lane == 2 * hp + hh, fq_all, 0.0), axis=1, keepdims=True)

        def step(ki, carry, diag):
            off = pl.multiple_of(ki * tq, tq)
            k = kb_ref[pl.ds(off, tq), :]
            v = vb_ref[pl.ds(off, tq), :]
            s = _dot_nt(qh, k) + (fq - fk_ref[hh:hh + 1, pl.ds(off, tq)])
            if diag:
                s = jnp.where(causal, s, NEG)
            return _softmax_step(s, v, *carry)

        z1 = jnp.zeros((tq, 1), F32)
        carry = lax.fori_loop(0, qi, lambda i, c: step(i, c, False),
                              (z1 + NEG, z1, jnp.zeros((tq, LANES), F32)))
        m, l, acc = step(qi, carry, True)
        outs.append(acc / l)
    o_ref[...] = jnp.where(lane < 64, outs[0], outs[1]).astype(o_ref.dtype)


def _prompt_attn(body, h_all, *, nb, L, n_groups, qcol, kcol, vcol, tq, extra=(), extra_specs=(), name):
    nq = L // tq
    in_specs = list(extra_specs) + [
        pl.BlockSpec((tq, LANES), lambda b, g, i: (b * nq + i, qcol + g)),
        pl.BlockSpec((L, LANES), lambda b, g, i: (b, kcol + g)),
        pl.BlockSpec((L, LANES), lambda b, g, i: (b, vcol + g))]
    return pl.pallas_call(
        body, grid=(nb, n_groups, nq), in_specs=in_specs,
        out_specs=pl.BlockSpec((tq, LANES), lambda b, g, i: (b * nq + i, g)),
        out_shape=jax.ShapeDtypeStruct((nb * L, n_groups * LANES), BF16),
        scratch_shapes=[pltpu.VMEM((L, LANES), BF16), pltpu.VMEM((L, LANES), BF16)],
        compiler_params=_params("parallel", "parallel", "arbitrary"), name=name)(*extra, h_all, h_all, h_all)


def _page_specs(block, layer, n_pages, reverse):
    specs = []
    for j in range(PAGES_PER_STEP):
        def idx(b, p, pt, j=j):
            lp = p * PAGES_PER_STEP + j
            if reverse:
                lp = n_pages - 1 - lp
            return (layer, pt[b, lp]) + (0,) * len(block)
        specs.append(pl.BlockSpec((None, None) + block, idx))
    return specs


def _softmax_update(s, pv, m_ref, l_ref, acc_ref):
    m = m_ref[...]
    m_new = jnp.maximum(m, jnp.max(s, axis=1, keepdims=True))
    alpha = jnp.exp(m - m_new)
    p = jnp.exp(s - m_new)
    m_ref[...] = m_new
    l_ref[...] = alpha * l_ref[...] + jnp.sum(p, axis=1, keepdims=True)
    acc_ref[...] = alpha * acc_ref[...] + pv(p.astype(BF16))


def _dec_diff_body(pt_ref, lv_ref, g_ref, q_ref, new_ref, *rest, lam_init):
    pages = rest[:PAGES_PER_STEP]
    o_ref, m_ref, l_ref, acc_ref = rest[PAGES_PER_STEP:]
    p_id = pl.program_id(1)
    q = q_ref[...]
    width = H_A * 2 * DK_A

    def update(k, v, mask):
        s = _dot_nt(q, k.astype(BF16))
        if mask is not None:
            s = jnp.where(mask, s, NEG)
        _softmax_update(s, lambda p: _dot(p, v.astype(BF16)), m_ref, l_ref, acc_ref)

    @pl.when(p_id == 0)
    def _():
        m_ref[...] = jnp.full_like(m_ref, NEG)
        l_ref[...] = jnp.zeros_like(l_ref)
        acc_ref[...] = jnp.zeros_like(acc_ref)
        i = _iota((LANES, PAGE), 0) % 16
        j = _iota((LANES, PAGE), 1)
        kv = new_ref[...]
        update(kv[:, :width], kv[:, width:], (j <= i) & (j < 8))

    def head_rows(pg, kv):
        return jnp.concatenate([pg[pl.ds(kv * H_A + h, PAGE, stride=2 * H_A), :] for h in range(H_A)], axis=1)

    for pg in pages:
        update(head_rows(pg, 0), head_rows(pg, 1), None)

    @pl.when(p_id == pl.num_programs(1) - 1)
    def _():
        lv = lv_ref[...]
        lam = (jnp.exp(jnp.sum(lv[0:1, :] * lv[1:2, :], axis=1, keepdims=True))
               - jnp.exp(jnp.sum(lv[2:3, :] * lv[3:4, :], axis=1, keepdims=True)) + lam_init)
        o_all = acc_ref[...] / l_ref[...]
        for h in range(H_A):
            r0 = (2 * h) * 16
            cs = slice(h * LANES, (h + 1) * LANES)
            o = o_all[r0:r0 + 16, cs] - lam * o_all[r0 + 16:r0 + 32, cs]
            o = o * lax.rsqrt(jnp.mean(o * o, axis=1, keepdims=True) + RMS_EPS) * g_ref[...]
            o_ref[:, cs] = o * (1.0 - lam_init)


def _dec_sb_body(pt_ref, q_ref, new_ref, *rest):
    pages = rest[:PAGES_PER_STEP]
    o_ref, c_ref, acc_ref = rest[PAGES_PER_STEP:]
    p_id = pl.program_id(1)
    q = q_ref[...]
    width = H_B * D_B
    later = (_iota((PAGE, PAGE), 0) > _iota((PAGE, PAGE), 1)).astype(BF16)

    def update(z, pv, mask):
        ls = _log_sigmoid(z)
        lom = ls - z
        if mask is not None:
            lom = jnp.where(mask, lom, 0.0)
        hi, lo = _split2(lom)
        suffix = _dot(hi, later) + _dot(lo, later)
        a = jnp.exp(ls + suffix + c_ref[...])
        if mask is not None:
            a = jnp.where(mask, a, 0.0)
        acc_ref[...] += pv(a.astype(BF16))
        c_ref[...] += jnp.sum(lom, axis=1, keepdims=True)

    @pl.when(p_id == 0)
    def _():
        c_ref[...] = jnp.zeros_like(c_ref)
        acc_ref[...] = jnp.zeros_like(acc_ref)
        i = _iota((LANES, PAGE), 0) % 16
        j = _iota((LANES, PAGE), 1)
        kv = new_ref[...]
        update(_dot_nt(q, kv[:, :width].astype(BF16)),
               lambda a: _dot(a, kv[:, width:].astype(BF16)), (j < i) & (j < 8))

    for pg in pages:
        update(_dot(q, pg[0].astype(BF16)), lambda a, pg=pg: _dot_nt(a, pg[1].astype(BF16)), None)

    @pl.when(p_id == pl.num_programs(1) - 1)
    def _():
        acc = acc_ref[...]
        col_head = _iota((16, width), 1) // D_B
        o = jnp.zeros((16, width), F32)
        for h in range(H_B):
            o = o + jnp.where(col_head == h, acc[h * 16:(h + 1) * 16, :], 0.0)
        o_ref[...] = o


def _dec_fox_body(pt_ref, q_ref, new_ref, lfnew_ref, *rest):
    pages = rest[:PAGES_PER_STEP]
    lfs = rest[PAGES_PER_STEP:2 * PAGES_PER_STEP]
    o_ref, m_ref, l_ref, acc_ref, d_ref = rest[2 * PAGES_PER_STEP:]
    p_id = pl.program_id(1)
    q = q_ref[...]
    width = H_C * D_C
    rows = _iota((PAGE, PAGE), 0)
    cols = _iota((PAGE, PAGE), 1)
    later = (rows > cols).astype(BF16)
    upto = (rows <= cols).astype(BF16)
    expand = ((cols < 3 * H_C) & (cols % H_C == rows // 8)).astype(BF16)

    def head_table(lf_t, tri):
        parts = jnp.concatenate(_split3(lf_t), axis=0)
        d3 = _dot(parts, tri)
        return d3[0:H_C] + d3[H_C:2 * H_C] + d3[2 * H_C:3 * H_C]

    def to_rows(tab):
        parts = jnp.concatenate(_split3(tab) + (jnp.zeros((PAGE - 3 * H_C, PAGE), BF16),), axis=0)
        return _dot(expand, parts)

    @pl.when(p_id == 0)
    def _():
        m_ref[...] = jnp.full_like(m_ref, NEG)
        l_ref[...] = jnp.zeros_like(l_ref)
        acc_ref[...] = jnp.zeros_like(acc_ref)
        d_ref[...] = jnp.zeros_like(d_ref)
        i = rows % 8
        kv = new_ref[...]
        s = _dot_nt(q, kv[:, :width].astype(BF16)) - to_rows(head_table(lfnew_ref[...], upto))
        s = jnp.where((cols <= i) & (cols < 8), s, NEG)
        _softmax_update(s, lambda p: _dot(p, kv[:, width:].astype(BF16)), m_ref, l_ref, acc_ref)

    for pg, lf in zip(pages, lfs):
        lf_t = lf[...]
        tab = head_table(lf_t, later) + d_ref[...]
        d_ref[...] += jnp.sum(lf_t, axis=1, keepdims=True)
        s = _dot(q, pg[0].astype(BF16)) + to_rows(tab)
        _softmax_update(s, lambda p, pg=pg: _dot_nt(p, pg[1].astype(BF16)), m_ref, l_ref, acc_ref)

    @pl.when(p_id == pl.num_programs(1) - 1)
    def _():
        o_all = acc_ref[...] / l_ref[...]
        col_head = _iota((8, width), 1) // D_C
        o = jnp.zeros((8, width), F32)
        for h in range(H_C):
            o = o + jnp.where(col_head == h, o_all[h * 8:(h + 1) * 8, :], 0.0)
        o_ref[...] = o


def _decode_attn(body, page_table, q_bd, new_kv, cache, layer, *, out_rows, out_cols, scratch,
                 reverse, extra=(), extra_specs=(), lf_new=None, lf_cache=None, name):
    nb, n_pages = page_table.shape
    steps = n_pages // PAGES_PER_STEP
    in_specs = list(extra_specs) + [
        pl.BlockSpec((None,) + q_bd.shape[1:], lambda b, p, pt: (b, 0, 0)),
        pl.BlockSpec((None,) + new_kv.shape[1:], lambda b, p, pt: (b, 0, 0))]
    args = list(extra) + [q_bd, new_kv]
    if lf_new is not None:
        in_specs.append(pl.BlockSpec((None,) + lf_new.shape[1:], lambda b, p, pt: (b, 0, 0)))
        args.append(lf_new)
    in_specs += _page_specs(cache.shape[2:], layer, n_pages, reverse)
    args += [cache] * PAGES_PER_STEP
    if lf_cache is not None:
        in_specs += _page_specs((H_C, PAGE), layer, n_pages, reverse)
        args += [lf_cache] * PAGES_PER_STEP
    grid_spec = pltpu.PrefetchScalarGridSpec(
        num_scalar_prefetch=1, grid=(nb, steps), in_specs=in_specs,
        out_specs=pl.BlockSpec((None, out_rows, out_cols), lambda b, p, pt: (b, 0, 0)),
        scratch_shapes=scratch)
    return pl.pallas_call(
        body, grid_spec=grid_spec,
        out_shape=jax.ShapeDtypeStruct((nb, out_rows, out_cols), F32),
        compiler_params=_params("parallel", "arbitrary"), name=name)(page_table, *args)


def _mem_body(q_ref, kv_ref, o_ref, kvb_ref, *, d_m):
    @pl.when(pl.program_id(1) == 0)
    def _():
        kvb_ref[...] = kv_ref[...].astype(BF16)

    hd = H_M * d_m
    for h in range(H_M):
        q = (q_ref[:, h * d_m:(h + 1) * d_m] * (d_m ** -0.5)).astype(BF16)
        s = _dot_nt(q, kvb_ref[:, h * d_m:(h + 1) * d_m])
        p = jnp.exp(s - jnp.max(s, axis=1, keepdims=True))
        o = _dot(p.astype(BF16), kvb_ref[:, hd + h * d_m: hd + (h + 1) * d_m])
        o_ref[:, h * d_m:(h + 1) * d_m] = (o / jnp.sum(p, axis=1, keepdims=True)).astype(o_ref.dtype)


def _mem_attn(q, mem_kv, kv_lead, *, nb, tq, name):
    M, D = q.shape
    nq = M // nb // tq
    n_mem = mem_kv.shape[-2]
    lead = tuple(kv_lead)
    return pl.pallas_call(
        functools.partial(_mem_body, d_m=D // H_M), grid=(nb, nq),
        in_specs=[pl.BlockSpec((tq, D), lambda b, i: (b * nq + i, 0)),
                  pl.BlockSpec((None,) * (len(lead) + 1) + (n_mem, 2 * D), lambda b, i: lead + (b, 0, 0))],
        out_specs=pl.BlockSpec((tq, D), lambda b, i: (b * nq + i, 0)),
        out_shape=jax.ShapeDtypeStruct((M, D), BF16),
        scratch_shapes=[pltpu.VMEM((n_mem, 2 * D), BF16)],
        compiler_params=_params("parallel", "arbitrary"), name=name)(q, mem_kv)


def _mm_ln_body(*refs, n_in, alpha):
    xs = refs[:n_in]
    ws = refs[n_in:2 * n_in]
    r_ref, g_ref, b_ref, o_ref = refs[2 * n_in:]
    y = _dot(xs[0][...].astype(BF16), ws[0][...])
    for x_ref, w_ref in zip(xs[1:], ws[1:]):
        y = y + _dot(x_ref[...].astype(BF16), w_ref[...])
    o_ref[...] = _layer_norm(alpha * r_ref[...] + y, g_ref[...], b_ref[...])


def _mm_ln(xs, w, layer, res, g, b, *, tm, alpha, name):
    M, D = res.shape
    in_specs, args, off = [], [], 0
    for x in xs:
        in_specs.append(pl.BlockSpec((tm, x.shape[1]), lambda m: (m, 0)))
    for x in xs:
        kx = x.shape[1]
        in_specs.append(pl.BlockSpec((None, kx, D), lambda m, o=off // kx: (layer, o, 0)))
        off += kx
    in_specs += [pl.BlockSpec((tm, D), lambda m: (m, 0)),
                 pl.BlockSpec((1, D), lambda m: (0, 0)), pl.BlockSpec((1, D), lambda m: (0, 0))]
    return pl.pallas_call(
        functools.partial(_mm_ln_body, n_in=len(xs), alpha=alpha), grid=(M // tm,),
        in_specs=in_specs, out_specs=pl.BlockSpec((tm, D), lambda m: (m, 0)),
        out_shape=jax.ShapeDtypeStruct((M, D), F32),
        compiler_params=_params("parallel"), name=name)(*xs, *([w] * len(xs)), res, g, b)


def _swiglu_chunk(xb, wg_ref, wu_ref, wd_ref):
    hg = _dot(xb, wg_ref[...])
    hu = _dot(xb, wu_ref[...])
    a = hg * jax.nn.sigmoid(hg) * hu
    return _dot(a.astype(BF16), wd_ref[...])


def _ffn_body(x_ref, wg_ref, wu_ref, wd_ref, g_ref, b_ref, o_ref, xb_ref, acc_ref, *, alpha):
    f = pl.program_id(1)

    @pl.when(f == 0)
    def _():
        xb_ref[...] = x_ref[...].astype(BF16)
        acc_ref[...] = jnp.zeros_like(acc_ref)

    acc_ref[...] += _swiglu_chunk(xb_ref[...], wg_ref, wu_ref, wd_ref)

    @pl.when(f == pl.num_programs(1) - 1)
    def _():
        o_ref[...] = _layer_norm(alpha * x_ref[...] + acc_ref[...], g_ref[...], b_ref[...])


def _ffn(x, w_gu, w_down, layer, g, b, *, tm, tf, alpha, name):
    M, D = x.shape
    nf = w_down.shape[1] // tf
    return pl.pallas_call(
        functools.partial(_ffn_body, alpha=alpha), grid=(M // tm, nf),
        in_specs=[pl.BlockSpec((tm, D), lambda m, f: (m, 0)),
                  pl.BlockSpec((None, D, tf), lambda m, f: (layer, 0, f)),
                  pl.BlockSpec((None, D, tf), lambda m, f: (layer, 0, nf + f)),
                  pl.BlockSpec((None, tf, D), lambda m, f: (layer, f, 0)),
                  pl.BlockSpec((1, D), lambda m, f: (0, 0)), pl.BlockSpec((1, D), lambda m, f: (0, 0))],
        out_specs=pl.BlockSpec((tm, D), lambda m, f: (m, 0)),
        out_shape=jax.ShapeDtypeStruct((M, D), F32),
        scratch_shapes=[pltpu.VMEM((tm, D), BF16), pltpu.VMEM((tm, D), F32)],
        compiler_params=_params("parallel", "arbitrary"), name=name)(x, w_gu, w_gu, w_down, g, b)


def _moe_body(x_ref, wr_ref, wg_ref, wu_ref, wd_ref, g_ref, b_ref, o_ref,
              xb_ref, comb_ref, acc_e_ref, acc_ref, *, alpha):
    e = pl.program_id(1)
    f = pl.program_id(2)
    last_f = f == pl.num_programs(2) - 1

    @pl.when((e == 0) & (f == 0))
    def _():
        x = x_ref[...]
        xb_ref[...] = x.astype(BF16)
        acc_ref[...] = jnp.zeros_like(acc_ref)
        logits = jnp.dot(x, wr_ref[...], precision=lax.Precision.HIGHEST, preferred_element_type=F32)
        lane = _iota(logits.shape, 1).astype(F32)
        logits = jnp.where(lane < N_EXPERTS, logits, NEG)
        m1 = jnp.max(logits, axis=1, keepdims=True)
        i1 = jnp.min(jnp.where(logits == m1, lane, float(LANES)), axis=1, keepdims=True)
        rest = jnp.where(lane == i1, NEG, logits)
        m2 = jnp.max(rest, axis=1, keepdims=True)
        i2 = jnp.min(jnp.where(rest == m2, lane, float(LANES)), axis=1, keepdims=True)
        e2 = jnp.exp(m2 - m1)
        den = 1.0 + e2
        comb_ref[...] = jnp.where(lane == i1, 1.0 / den, 0.0) + jnp.where(lane == i2, e2 / den, 0.0)

    @pl.when(f == 0)
    def _():
        acc_e_ref[...] = jnp.zeros_like(acc_e_ref)

    acc_e_ref[...] += _swiglu_chunk(xb_ref[...], wg_ref, wu_ref, wd_ref)

    @pl.when(last_f)
    def _():
        comb = comb_ref[...]
        gate = jnp.sum(jnp.where(_iota(comb.shape, 1) == e, comb, 0.0), axis=1, keepdims=True)
        acc_ref[...] += gate * acc_e_ref[...]

    @pl.when(last_f & (e == pl.num_programs(1) - 1))
    def _():
        o_ref[...] = _layer_norm(alpha * x_ref[...] + acc_ref[...], g_ref[...], b_ref[...])


def _moe(x, w_router, w_gu, w_down, layer, g, b, *, tm, tf, alpha, name):
    M, D = x.shape
    nf = w_down.shape[2] // tf
    return pl.pallas_call(
        functools.partial(_moe_body, alpha=alpha), grid=(M // tm, N_EXPERTS, nf),
        in_specs=[pl.BlockSpec((tm, D), lambda m, e, f: (m, 0)),
                  pl.BlockSpec((None, D, LANES), lambda m, e, f: (layer, 0, 0)),
                  pl.BlockSpec((None, None, D, tf), lambda m, e, f: (layer, e, 0, f)),
                  pl.BlockSpec((None, None, D, tf), lambda m, e, f: (layer, e, 0, nf + f)),
                  pl.BlockSpec((None, None, tf, D), lambda m, e, f: (layer, e, f, 0)),
                  pl.BlockSpec((1, D), lambda m, e, f: (0, 0)), pl.BlockSpec((1, D), lambda m, e, f: (0, 0))],
        out_specs=pl.BlockSpec((tm, D), lambda m, e, f: (m, 0)),
        out_shape=jax.ShapeDtypeStruct((M, D), F32),
        scratch_shapes=[pltpu.VMEM((tm, D), BF16), pltpu.VMEM((tm, LANES), F32),
                        pltpu.VMEM((tm, D), F32), pltpu.VMEM((tm, D), F32)],
        compiler_params=_params("parallel", "arbitrary", "arbitrary"), name=name)(
            x, w_router, w_gu, w_gu, w_down, g, b)


def _rope_tables(pos):
    half = DK_A // 2
    inv = 1.0 / (ROPE_THETA ** (jnp.arange(half, dtype=F32) / half))
    ang = pos.astype(F32)[:, None] * inv[None, :]
    cos, sin = jnp.cos(ang), jnp.sin(ang)
    return jnp.concatenate([cos] * 4, axis=1), jnp.concatenate([-sin, sin, -sin, sin], axis=1)


def _block_diag_queries(q, nb, n_heads, n_maps, d, rows_per_head, scale):
    g = n_heads * n_maps
    lq = q.shape[0] // nb
    q5 = (q * scale).reshape(nb, lq, g, d)
    q5 = jnp.pad(q5, ((0, 0), (0, rows_per_head - lq), (0, 0), (0, 0)))
    eye = jnp.eye(g, dtype=q.dtype)
    out = jnp.einsum('bigd,gx->bgixd', q5, eye)
    return out.reshape(nb, g * rows_per_head, g * d).astype(BF16)


def _pad_rows(x, nb, rows):
    lq = x.shape[0] // nb
    return jnp.pad(x.reshape(nb, lq, x.shape[1]), ((0, 0), (0, rows - lq), (0, 0)))


def kernel(x_prompt, x_sample, cache_a_kv, cache_b_kv, cache_c_kv, cache_c_logf, cache_mem_kv,
           page_table, mem_prompt, even_w_in, even_w_out, diff_lambda, diff_subln_g,
           odd_w_in, odd_b_f, odd_w_out, mem_w_q, mem_w_kv, mem_w_o, ffn_w_gu, ffn_w_down,
           moe_w_router, moe_w_gu, moe_w_down, ln_g, ln_b):
    depth = ln_g.shape[0]
    alpha = (2 * depth) ** 0.25
    nbp, L, D = x_prompt.shape
    nbs, Ls, _ = x_sample.shape
    n_pages = page_table.shape[1]
    n_pool = cache_a_kv.shape[1]
    past_len = n_pages * PAGE
    n_mem = mem_prompt.shape[1]
    assert Ls == 8 and n_pages % PAGES_PER_STEP == 0
    qa_w = H_A * 2 * DK_A
    qb_w = H_B * D_B
    qc_w = H_C * D_C
    even_in = 3 * qa_w + 3 * qb_w
    odd_main = 3 * qc_w

    w_even_in = even_w_in.astype(BF16)
    w_even_out = even_w_out.astype(BF16)
    w_odd_in = odd_w_in[:, :, :odd_main].astype(BF16)
    w_odd_f = jnp.pad(odd_w_in[:, :, odd_main:], ((0, 0), (0, 0), (0, LANES - H_C))).astype(BF16)
    b_odd_f = jnp.pad(odd_b_f, ((0, 0), (0, LANES - H_C)))[:, None, :]
    w_odd_out = odd_w_out.astype(BF16)
    w_mem_q = mem_w_q.astype(BF16)
    w_mem_kv = mem_w_kv.astype(BF16)
    w_mem_o = mem_w_o.astype(BF16)
    w_ffn_gu = ffn_w_gu.astype(BF16)
    w_ffn_down = ffn_w_down.astype(BF16)
    w_moe_gu = moe_w_gu.astype(BF16)
    w_moe_down = moe_w_down.astype(BF16)
    w_router = jnp.pad(moe_w_router, ((0, 0), (0, 0), (0, LANES - N_EXPERTS)))

    ca = cache_a_kv.reshape(cache_a_kv.shape[0], n_pool, PAGE * 2 * H_A, 2 * DK_A)
    cb = jnp.transpose(cache_b_kv, (0, 1, 3, 4, 5, 2)).reshape(cache_b_kv.shape[0], n_pool, 2, qb_w, PAGE)
    cc = jnp.transpose(cache_c_kv, (0, 1, 3, 4, 5, 2)).reshape(cache_c_kv.shape[0], n_pool, 2, qc_w, PAGE)
    clf = jnp.swapaxes(cache_c_logf, 2, 3)

    rope_p = _rope_tables(jnp.arange(L))
    rope_s = _rope_tables(past_len + (jnp.arange(nbs * Ls) % Ls))

    xp = x_prompt.reshape(nbp * L, D)
    xs = x_sample.reshape(nbs * Ls, D)
    mp = mem_prompt.reshape(nbp * n_mem, D)
    Mp, Ms = xp.shape[0], xs.shape[0]
    tmp, tms = min(512, Mp), Ms
    tq = min(256, L)

    a_p, a_s, b_p, b_s, c_p, c_s, lf_p, lf_s, mem_rows = [], [], [], [], [], [], [], [], []
    for l in range(depth):
        i = l // 2
        g3 = ln_g[l][:, None, :]
        b3 = ln_b[l][:, None, :]
        mem_kv_p = _proj(mp, w_mem_kv, l, 2 * D, tm=min(512, mp.shape[0]), name=f"memkv{l}")
        mem_rows.append(mem_kv_p.reshape(nbp, n_mem, 2, H_M, D // H_M))
        mem_kv_p = mem_kv_p.reshape(nbp, n_mem, 2 * D)
        if l % 2 == 0:
            lam_init = _lambda_init(l)
            lv = diff_lambda[i]
            sg = diff_subln_g[i][None, :]
            hp = _proj(xp, w_even_in, i, even_in, tm=tmp, rope=rope_p, n_rope=2, name=f"even_in_p{l}")
            a_p.append(hp[:, qa_w:3 * qa_w].reshape(nbp, L, 2, H_A, 2 * DK_A))
            b_p.append(hp[:, 3 * qa_w + qb_w:].reshape(nbp, L, 2, H_B, D_B))
            oa = _prompt_attn(functools.partial(_diff_body, tq=tq, lam_init=lam_init), hp, nb=nbp, L=L,
                              n_groups=H_A, qcol=0, kcol=H_A, vcol=2 * H_A, tq=tq, extra=(lv, sg),
                              extra_specs=(pl.BlockSpec(lv.shape, lambda b, g, i: (0, 0)),
                                           pl.BlockSpec(sg.shape, lambda b, g, i: (0, 0))),
                              name=f"diff_p{l}")
            ob = _prompt_attn(functools.partial(_sb_body, tq=tq), hp, nb=nbp, L=L, n_groups=H_B // 2,
                              qcol=3 * H_A, kcol=3 * H_A + H_B // 2, vcol=3 * H_A + H_B, tq=tq,
                              name=f"stick_p{l}")
            xp = _mm_ln([oa, ob], w_even_out, i, xp, g3[0], b3[0], tm=tmp, alpha=alpha, name=f"even_out_p{l}")
            hs = _proj(xs, w_even_in, i, even_in, tm=tms, rope=rope_s, n_rope=2, name=f"even_in_s{l}")
            a_new = hs[:, qa_w:3 * qa_w]
            b_new = hs[:, 3 * qa_w + qb_w:]
            a_s.append(a_new.reshape(nbs, Ls, 2, H_A, 2 * DK_A))
            b_s.append(b_new.reshape(nbs, Ls, 2, H_B, D_B))
            qa_bd = _block_diag_queries(hs[:, :qa_w], nbs, H_A, 2, DK_A, 16, DK_A ** -0.5)
            qb_bd = _block_diag_queries(hs[:, 3 * qa_w:3 * qa_w + qb_w], nbs, H_B, 1, D_B, 16, D_B ** -0.5)
            oa = _decode_attn(functools.partial(_dec_diff_body, lam_init=lam_init), page_table, qa_bd,
                              _pad_rows(a_new, nbs, PAGE), ca, i, out_rows=16, out_cols=qa_w,
                              scratch=[pltpu.VMEM((LANES, 1), F32), pltpu.VMEM((LANES, 1), F32),
                                       pltpu.VMEM((LANES, qa_w), F32)],
                              reverse=False, extra=(lv, sg),
                              extra_specs=(pl.BlockSpec(lv.shape, lambda b, p, pt: (0, 0)),
                                           pl.BlockSpec(sg.shape, lambda b, p, pt: (0, 0))),
                              name=f"diff_s{l}")
            ob = _decode_attn(_dec_sb_body, page_table, qb_bd, _pad_rows(b_new, nbs, PAGE), cb, i,
                              out_rows=16, out_cols=qb_w,
                              scratch=[pltpu.VMEM((LANES, 1), F32), pltpu.VMEM((LANES, qb_w), F32)],
                              reverse=True, name=f"stick_s{l}")
            oa = oa[:, :Ls].reshape(Ms, qa_w)
            ob = ob[:, :Ls].reshape(Ms, qb_w)
            xs = _mm_ln([oa, ob], w_even_out, i, xs, g3[0], b3[0], tm=tms, alpha=alpha, name=f"even_out_s{l}")
        else:
            hp = _proj(xp, w_odd_in, i, odd_main, tm=tmp, name=f"odd_in_p{l}")
            c_p.append(hp[:, qc_w:].reshape(nbp, L, 2, H_C, D_C))
            lf, cs = _logf(xp, w_odd_f[i], b_odd_f[i], nb=nbp, tl=tq, with_cumsum=True, name=f"logf_p{l}")
            lf_p.append(lf[:, :H_C].reshape(nbp, L, H_C))
            fk = jnp.swapaxes(cs[:, :H_C].reshape(nbp, L, H_C // 2, 2), 1, 2)
            fk = jnp.swapaxes(fk, 2, 3)
            nq = L // tq
            oc = _prompt_attn(functools.partial(_fox_body, tq=tq), hp, nb=nbp, L=L, n_groups=H_C // 2,
                              qcol=0, kcol=H_C // 2, vcol=H_C, tq=tq, extra=(cs, fk),
                              extra_specs=(pl.BlockSpec((tq, LANES), lambda b, g, i: (b * nq + i, 0)),
                                           pl.BlockSpec((None, None, 2, L), lambda b, g, i: (b, g, 0, 0))),
                              name=f"fox_p{l}")
            xp = _mm_ln([oc], w_odd_out, i, xp, g3[0], b3[0], tm=tmp, alpha=alpha, name=f"odd_out_p{l}")
            hs = _proj(xs, w_odd_in, i, odd_main, tm=tms, name=f"odd_in_s{l}")
            c_new = hs[:, qc_w:]
            c_s.append(c_new.reshape(nbs, Ls, 2, H_C, D_C))
            lf, _ = _logf(xs, w_odd_f[i], b_odd_f[i], nb=1, tl=Ms, with_cumsum=False, name=f"logf_s{l}")
            lf_s.append(lf[:, :H_C].reshape(nbs, Ls, H_C))
            lf_new_t = jnp.swapaxes(_pad_rows(lf[:, :H_C], nbs, PAGE), 1, 2)
            qc_bd = _block_diag_queries(hs[:, :qc_w], nbs, H_C, 1, D_C, 8, D_C ** -0.5)
            oc = _decode_attn(_dec_fox_body, page_table, qc_bd, _pad_rows(c_new, nbs, PAGE), cc, i,
                              out_rows=8, out_cols=qc_w,
                              scratch=[pltpu.VMEM((LANES, 1), F32), pltpu.VMEM((LANES, 1), F32),
                                       pltpu.VMEM((LANES, qc_w), F32), pltpu.VMEM((H_C, 1), F32)],
                              reverse=True, lf_new=lf_new_t, lf_cache=clf, name=f"fox_s{l}")
            oc = oc.reshape(Ms, qc_w)
            xs = _mm_ln([oc], w_odd_out, i, xs, g3[0], b3[0], tm=tms, alpha=alpha, name=f"odd_out_s{l}")

        qm = _proj(xp, w_mem_q, l, D, tm=tmp, name=f"mem_q_p{l}")
        om = _mem_attn(qm, mem_kv_p, (), nb=nbp, tq=tq, name=f"mem_attn_p{l}")
        xp = _mm_ln([om], w_mem_o, l, xp, g3[1], b3[1], tm=tmp, alpha=alpha, name=f"mem_out_p{l}")
        qm = _proj(xs, w_mem_q, l, D, tm=tms, name=f"mem_q_s{l}")
        om = _mem_attn(qm, cache_mem_kv.reshape(depth, nbs, n_mem, 2 * D), (l,), nb=nbs, tq=Ls,
                       name=f"mem_attn_s{l}")
        xs = _mm_ln([om], w_mem_o, l, xs, g3[1], b3[1], tm=tms, alpha=alpha, name=f"mem_out_s{l}")

        if l % 2 == 0:
            xp = _ffn(xp, w_ffn_gu, w_ffn_down, i, g3[2], b3[2], tm=min(1024, Mp), tf=256, alpha=alpha, name=f"ffn_p{l}")
            xs = _ffn(xs, w_ffn_gu, w_ffn_down, i, g3[2], b3[2], tm=tms, tf=256, alpha=alpha, name=f"ffn_s{l}")
        else:
            xp = _moe(xp, w_router, w_moe_gu, w_moe_down, i, g3[2], b3[2], tm=min(1024, Mp), tf=256,
                      alpha=alpha, name=f"moe_p{l}")
            xs = _moe(xs, w_router, w_moe_gu, w_moe_down, i, g3[2], b3[2], tm=tms, tf=256, alpha=alpha,
                      name=f"moe_s{l}")

    return (xp.reshape(nbp, L, D), xs.reshape(nbs, Ls, D),
            jnp.stack(a_p), jnp.stack(a_s), jnp.stack(b_p), jnp.stack(b_s),
            jnp.stack(c_p), jnp.stack(c_s), jnp.stack(lf_p), jnp.stack(lf_s), jnp.stack(mem_rows))
```

```python
import functools
import math

import jax
import jax.numpy as jnp
from jax import lax
from jax.experimental import pallas as pl
from jax.experimental.pallas import tpu as pltpu

F32 = jnp.float32
BF16 = jnp.bfloat16

H_A, DK_A = 4, 64
H_B, D_B = 8, 64
H_C, D_C = 16, 64
H_M = 4
N_EXPERTS, TOP_K = 8, 2
ROPE_THETA = 10000.0
LN_EPS = 1e-5
RMS_EPS = 1e-5
PAGE = 128
LANES = 128
NEG = -1e30
SB_CUTOFF = -104.0
VMEM_LIMIT = 56 << 20
PAGES_PER_STEP = 8
MOE_CHUNK = 128


def _lambda_init(layer_idx):
    return 0.8 - 0.6 * math.exp(-0.3 * layer_idx)


def _params(*sem):
    return pltpu.CompilerParams(dimension_semantics=sem, vmem_limit_bytes=VMEM_LIMIT)


def _dot(a, b):
    return jnp.dot(a, b, preferred_element_type=F32)


def _dot_nt(a, b):
    return lax.dot_general(a, b, (((1,), (1,)), ((), ())), preferred_element_type=F32)


def _log_sigmoid(z):
    return jnp.minimum(z, 0.0) - jnp.log1p(jnp.exp(-jnp.abs(z)))


def _layer_norm(v, g, b):
    mu = jnp.mean(v, axis=1, keepdims=True)
    d = v - mu
    var = jnp.mean(d * d, axis=1, keepdims=True)
    return d * lax.rsqrt(var + LN_EPS) * g + b


def _split2(x):
    hi = x.astype(BF16)
    lo = (x - hi.astype(F32)).astype(BF16)
    return hi, lo


def _split3(x):
    h1 = x.astype(BF16)
    r1 = x - h1.astype(F32)
    h2 = r1.astype(BF16)
    h3 = (r1 - h2.astype(F32)).astype(BF16)
    return h1, h2, h3


def _iota(shape, dim):
    return lax.broadcasted_iota(jnp.int32, shape, dim)


def _proj_body(*refs, n_rope):
    if n_rope:
        x_ref, w_ref, cos_ref, sin_ref, o_ref = refs
    else:
        x_ref, w_ref, o_ref = refs
    h = _dot(x_ref[...].astype(BF16), w_ref[...])
    if not n_rope:
        o_ref[...] = h
        return
    n = pl.program_id(1)
    tn = h.shape[1]

    @pl.when(n < n_rope)
    def _():
        reps = tn // LANES
        c = jnp.concatenate([cos_ref[...]] * reps, axis=1)
        s = jnp.concatenate([sin_ref[...]] * reps, axis=1)
        lane = _iota(h.shape, 1)
        first = (lane % 64) < 32
        rot = jnp.where(first, pltpu.roll(h, tn - 32, 1), pltpu.roll(h, 32, 1))
        o_ref[...] = h * c + rot * s

    @pl.when(n >= n_rope)
    def _():
        o_ref[...] = h


def _proj(x, w, layer, n_cols, *, tm, tn=512, rope=None, n_rope=0, name):
    M, K = x.shape
    grid = (M // tm, n_cols // tn)
    in_specs = [pl.BlockSpec((tm, K), lambda m, n: (m, 0)),
                pl.BlockSpec((None, K, tn), lambda m, n: (layer, 0, n))]
    args = [x, w]
    if n_rope:
        cos, sin = rope
        nt = cos.shape[0] // tm
        in_specs += [pl.BlockSpec((tm, LANES), lambda m, n: (m % nt, 0)),
                     pl.BlockSpec((tm, LANES), lambda m, n: (m % nt, 0))]
        args += [cos, sin]
    return pl.pallas_call(
        functools.partial(_proj_body, n_rope=n_rope),
        grid=grid, in_specs=in_specs,
        out_specs=pl.BlockSpec((tm, tn), lambda m, n: (m, n)),
        out_shape=jax.ShapeDtypeStruct((M, n_cols), F32),
        compiler_params=_params("parallel", "arbitrary"), name=name)(*args)


def _logf_body(x_ref, w_ref, b_ref, lf_ref, cs_ref, carry_ref, *, with_cumsum):
    t = pl.program_id(1)

    @pl.when(t == 0)
    def _():
        carry_ref[...] = jnp.zeros_like(carry_ref)

    f = _dot(x_ref[...].astype(BF16), w_ref[...]) + b_ref[...]
    lf = _log_sigmoid(f)
    lf_ref[...] = lf
    tl = lf.shape[0]
    if not with_cumsum:
        cs_ref[...] = lf
        return
    tri = (_iota((tl, tl), 1) <= _iota((tl, tl), 0)).astype(BF16)
    parts = _split3(lf)
    cs = carry_ref[...] + _dot(tri, parts[0]) + _dot(tri, parts[1]) + _dot(tri, parts[2])
    cs_ref[...] = cs
    carry_ref[...] = cs[tl - 1:tl, :]


def _logf(x, wf, bf, *, nb, tl, with_cumsum, name):
    M, K = x.shape
    nt = M // nb // tl
    return pl.pallas_call(
        functools.partial(_logf_body, with_cumsum=with_cumsum), grid=(nb, nt),
        in_specs=[pl.BlockSpec((tl, K), lambda b, t: (b * nt + t, 0)),
                  pl.BlockSpec((K, LANES), lambda b, t: (0, 0)),
                  pl.BlockSpec((1, LANES), lambda b, t: (0, 0))],
        out_specs=[pl.BlockSpec((tl, LANES), lambda b, t: (b * nt + t, 0)),
                   pl.BlockSpec((tl, LANES), lambda b, t: (b * nt + t, 0))],
        out_shape=[jax.ShapeDtypeStruct((M, LANES), F32)] * 2,
        scratch_shapes=[pltpu.VMEM((1, LANES), F32)],
        compiler_params=_params("parallel", "arbitrary"), name=name)(x, wf, bf)


def _load_kv(k_ref, v_ref, kb_ref, vb_ref):
    @pl.when(pl.program_id(2) == 0)
    def _():
        kb_ref[...] = k_ref[...].astype(BF16)
        vb_ref[...] = v_ref[...].astype(BF16)


def _softmax_step(s, v, m, l, acc):
    m_new = jnp.maximum(m, jnp.max(s, axis=1, keepdims=True))
    alpha = jnp.exp(m - m_new)
    p = jnp.exp(s - m_new)
    l = alpha * l + jnp.sum(p, axis=1, keepdims=True)
    acc = alpha * acc + _dot(p.astype(BF16), v)
    return m_new, l, acc


def _key_blocks(qi, tq, tk):
    return ((qi + 1) * tq + tk - 1) // tk


def _causal_mask(qi, kb, tq, tk):
    return kb * tk + _iota((tq, tk), 1) <= qi * tq + _iota((tq, tk), 0)


def _diff_body(lv_ref, g_ref, q_ref, k_ref, v_ref, o_ref, kb_ref, vb_ref, *, tq, tk, lam_init):
    _load_kv(k_ref, v_ref, kb_ref, vb_ref)
    qi = pl.program_id(2)
    q = q_ref[...] * (DK_A ** -0.5)
    lane = _iota(q.shape, 1)
    qs = (jnp.where(lane < 64, q, 0.0).astype(BF16), jnp.where(lane >= 64, q, 0.0).astype(BF16))

    def step(kb, carry, masked):
        off = pl.multiple_of(kb * tk, tk)
        k = kb_ref[pl.ds(off, tk), :]
        v = vb_ref[pl.ds(off, tk), :]
        out = []
        for c in range(2):
            s = _dot_nt(qs[c], k)
            if masked:
                s = jnp.where(_causal_mask(qi, kb, tq, tk), s, NEG)
            out.append(_softmax_step(s, v, *carry[c]))
        return tuple(out)

    z1 = jnp.zeros((tq, 1), F32)
    init = ((z1 + NEG, z1, jnp.zeros((tq, LANES), F32)),) * 2
    nk = _key_blocks(qi, tq, tk)
    carry = lax.fori_loop(0, nk - 1, lambda i, c: step(i, c, False), init)
    (m0, l0, a0), (m1, l1, a1) = step(nk - 1, carry, True)
    lv = lv_ref[...]
    lam = (jnp.exp(jnp.sum(lv[0:1, :] * lv[1:2, :], axis=1, keepdims=True))
           - jnp.exp(jnp.sum(lv[2:3, :] * lv[3:4, :], axis=1, keepdims=True)) + lam_init)
    o = a0 / l0 - lam * (a1 / l1)
    o = o * lax.rsqrt(jnp.mean(o * o, axis=1, keepdims=True) + RMS_EPS) * g_ref[...]
    o_ref[...] = (o * (1.0 - lam_init)).astype(o_ref.dtype)


def _sb_body(q_ref, k_ref, v_ref, o_ref, kb_ref, vb_ref, *, tq):
    _load_kv(k_ref, v_ref, kb_ref, vb_ref)
    qi = pl.program_id(2)
    q = q_ref[...] * (D_B ** -0.5)
    lane = _iota(q.shape, 1)
    rows = _iota((tq, tq), 0)
    cols = _iota((tq, tq), 1)
    later = (rows > cols).astype(BF16)
    strict = cols < rows
    qs = [jnp.where((lane >= 64 * hh) & (lane < 64 * hh + 64), q, 0.0).astype(BF16) for hh in range(2)]

    def step(ki, carry, diag):
        off = pl.multiple_of(ki * tq, tq)
        k = kb_ref[pl.ds(off, tq), :]
        v = vb_ref[pl.ds(off, tq), :]
        out = []
        for hh in range(2):
            c_run, acc = carry[hh]
            z = _dot_nt(qs[hh], k)
            ls = _log_sigmoid(z)
            lom = ls - z
            if diag:
                lom = jnp.where(strict, lom, 0.0)
            hi, lo = _split2(lom)
            suffix = _dot(hi, later) + _dot(lo, later)
            a = jnp.exp(ls + suffix + c_run)
            if diag:
                a = jnp.where(strict, a, 0.0)
            out.append((c_run + jnp.sum(lom, axis=1, keepdims=True), acc + _dot(a.astype(BF16), v)))
        return tuple(out)

    def live(carry):
        return jnp.max(jnp.maximum(carry[0][0], carry[1][0])) > SB_CUTOFF

    init = ((jnp.zeros((tq, 1), F32), jnp.zeros((tq, LANES), F32)),) * 2
    carry = step(qi, init, True)
    _, carry = lax.while_loop(lambda st: (st[0] < qi) & live(st[1]),
                              lambda st: (st[0] + 1, step(qi - 1 - st[0], st[1], False)),
                              (jnp.int32(0), carry))
    o_ref[...] = jnp.where(lane < 64, carry[0][1], carry[1][1]).astype(o_ref.dtype)


def _fox_extra_lanes(lane, hh, parts, ones_first):
    e0 = 64 * (1 - hh)
    one = jnp.where((lane >= e0 + (0 if ones_first else 3)) & (lane < e0 + (3 if ones_first else 6)), 1.0, 0.0)
    p0 = e0 + (3 if ones_first else 0)
    return (one + jnp.where(lane == p0, parts[0], 0.0) + jnp.where(lane == p0 + 1, parts[1], 0.0)
            + jnp.where(lane == p0 + 2, parts[2], 0.0))


def _split3_f32(x):
    h1 = x.astype(BF16).astype(F32)
    r1 = x - h1
    h2 = r1.astype(BF16).astype(F32)
    return h1, h2, (r1 - h2).astype(BF16).astype(F32)


def _fox_body(fq_ref, fk_ref, q_ref, k_ref, v_ref, o_ref, ka_ref, va_ref, *, tq, tk):
    g = pl.program_id(1)
    qi = pl.program_id(2)

    @pl.when(qi == 0)
    def _():
        k = k_ref[...]
        v = v_ref[...]
        f_all = fk_ref[...]
        lane = _iota(k.shape, 1)
        for hh in range(2):
            own = (lane >= 64 * hh) & (lane < 64 * hh + 64)
            f = jnp.sum(jnp.where(lane == 2 * g + hh, f_all, 0.0), axis=1, keepdims=True)
            parts = [-p for p in _split3_f32(f)]
            ka_ref[hh] = jnp.where(own, k, _fox_extra_lanes(lane, hh, parts, True)).astype(BF16)
            va_ref[hh] = jnp.where(own, v, jnp.where(lane == 64 * (1 - hh), 1.0, 0.0)).astype(BF16)

    q = q_ref[...] * (D_C ** -0.5)
    lane = _iota(q.shape, 1)
    fq_all = fq_ref[...]
    qs = []
    for hh in range(2):
        own = (lane >= 64 * hh) & (lane < 64 * hh + 64)
        fq = jnp.sum(jnp.where(lane == 2 * g + hh, fq_all, 0.0), axis=1, keepdims=True)
        qs.append(jnp.where(own, q, _fox_extra_lanes(lane, hh, _split3_f32(fq), False)).astype(BF16))

    def step(kb, carry, masked):
        off = pl.multiple_of(kb * tk, tk)
        out = []
        for hh in range(2):
            m, acc = carry[hh]
            s = _dot_nt(qs[hh], ka_ref[hh, pl.ds(off, tk), :])
            if masked:
                s = jnp.where(_causal_mask(qi, kb, tq, tk), s, NEG)
            m_new = jnp.maximum(m, jnp.max(s, axis=1, keepdims=True))
            p = jnp.exp(s - m_new).astype(BF16)
            out.append((m_new, jnp.exp(m - m_new) * acc + _dot(p, va_ref[hh, pl.ds(off, tk), :])))
        return tuple(out)

    init = ((jnp.full((tq, 1), NEG, F32), jnp.zeros((tq, LANES), F32)),) * 2
    nk = _key_blocks(qi, tq, tk)
    carry = lax.fori_loop(0, nk - 1, lambda i, c: step(i, c, False), init)
    (_, a0), (_, a1) = step(nk - 1, carry, True)
    o_ref[...] = jnp.where(lane < 64, a0 / a0[:, 64:65], a1 / a1[:, 0:1]).astype(o_ref.dtype)


def _prompt_attn(body, h_all, *, nb, L, n_groups, qcol, kcol, vcol, tq, scratch, extra=(), extra_specs=(), name):
    nq = L // tq
    in_specs = list(extra_specs) + [
        pl.BlockSpec((tq, LANES), lambda b, g, i: (b * nq + i, qcol + g)),
        pl.BlockSpec((L, LANES), lambda b, g, i: (b, kcol + g)),
        pl.BlockSpec((L, LANES), lambda b, g, i: (b, vcol + g))]
    return pl.pallas_call(
        body, grid=(nb, n_groups, nq), in_specs=in_specs,
        out_specs=pl.BlockSpec((tq, LANES), lambda b, g, i: (b * nq + i, g)),
        out_shape=jax.ShapeDtypeStruct((nb * L, n_groups * LANES), BF16),
        scratch_shapes=[pltpu.VMEM(scratch + (L, LANES), BF16), pltpu.VMEM(scratch + (L, LANES), BF16)],
        compiler_params=_params("parallel", "parallel", "arbitrary"), name=name)(*extra, h_all, h_all, h_all)


def _page_specs(block, layer, n_pages, reverse):
    specs = []
    for j in range(PAGES_PER_STEP):
        def idx(b, p, pt, j=j):
            lp = p * PAGES_PER_STEP + j
            if reverse:
                lp = n_pages - 1 - lp
            return (layer, pt[b, lp]) + (0,) * len(block)
        specs.append(pl.BlockSpec((None, None) + block, idx))
    return specs


def _softmax_update(s, pv, m_ref, l_ref, acc_ref):
    m = m_ref[...]
    m_new = jnp.maximum(m, jnp.max(s, axis=1, keepdims=True))
    alpha = jnp.exp(m - m_new)
    p = jnp.exp(s - m_new)
    m_ref[...] = m_new
    l_ref[...] = alpha * l_ref[...] + jnp.sum(p, axis=1, keepdims=True)
    acc_ref[...] = alpha * acc_ref[...] + pv(p.astype(BF16))


def _dec_diff_body(pt_ref, lv_ref, g_ref, q_ref, new_ref, *rest, lam_init):
    pages = rest[:PAGES_PER_STEP]
    o_ref, m_ref, l_ref, acc_ref = rest[PAGES_PER_STEP:]
    p_id = pl.program_id(1)
    q = q_ref[...]
    width = H_A * 2 * DK_A

    def update(ks, vs, mask):
        s = jnp.concatenate([_dot_nt(q, k.astype(BF16)) for k in ks], axis=1)
        if mask is not None:
            s = jnp.where(mask, s, NEG)

        def pv(p):
            out = _dot(p[:, :PAGE], vs[0].astype(BF16))
            for j in range(1, len(vs)):
                out = out + _dot(p[:, j * PAGE:(j + 1) * PAGE], vs[j].astype(BF16))
            return out

        _softmax_update(s, pv, m_ref, l_ref, acc_ref)

    @pl.when(p_id == 0)
    def _():
        m_ref[...] = jnp.full_like(m_ref, NEG)
        l_ref[...] = jnp.zeros_like(l_ref)
        acc_ref[...] = jnp.zeros_like(acc_ref)
        i = _iota((LANES, PAGE), 0) % 16
        j = _iota((LANES, PAGE), 1)
        kv = new_ref[...]
        update([kv[:, :width]], [kv[:, width:]], (j <= i) & (j < 8))

    def head_rows(pg, kv):
        return jnp.concatenate([pg[pl.ds(kv * H_A + h, PAGE, stride=2 * H_A), :] for h in range(H_A)], axis=1)

    update([head_rows(pg, 0) for pg in pages], [head_rows(pg, 1) for pg in pages], None)

    @pl.when(p_id == pl.num_programs(1) - 1)
    def _():
        lv = lv_ref[...]
        lam = (jnp.exp(jnp.sum(lv[0:1, :] * lv[1:2, :], axis=1, keepdims=True))
               - jnp.exp(jnp.sum(lv[2:3, :] * lv[3:4, :], axis=1, keepdims=True)) + lam_init)
        o_all = acc_ref[...] / l_ref[...]
        for h in range(H_A):
            r0 = (2 * h) * 16
            cs = slice(h * LANES, (h + 1) * LANES)
            o = o_all[r0:r0 + 16, cs] - lam * o_all[r0 + 16:r0 + 32, cs]
            o = o * lax.rsqrt(jnp.mean(o * o, axis=1, keepdims=True) + RMS_EPS) * g_ref[...]
            o_ref[:, cs] = o * (1.0 - lam_init)


def _dec_sb_body(pt_ref, q_ref, new_ref, *rest):
    pages = rest[:PAGES_PER_STEP]
    o_ref, c_ref, acc_ref = rest[PAGES_PER_STEP:]
    p_id = pl.program_id(1)
    q = q_ref[...]
    width = H_B * D_B
    later = (_iota((PAGE, PAGE), 0) > _iota((PAGE, PAGE), 1)).astype(BF16)

    def update(zs, pvs, mask):
        c_run = c_ref[...]
        total = None
        for z, pv in zip(zs, pvs):
            ls = _log_sigmoid(z)
            lom = ls - z
            if mask is not None:
                lom = jnp.where(mask, lom, 0.0)
            hi, lo = _split2(lom)
            suffix = _dot(hi, later) + _dot(lo, later)
            a = jnp.exp(ls + suffix + c_run)
            if mask is not None:
                a = jnp.where(mask, a, 0.0)
            out = pv(a.astype(BF16))
            total = out if total is None else total + out
            c_run = c_run + jnp.sum(lom, axis=1, keepdims=True)
        acc_ref[...] += total
        c_ref[...] = c_run

    @pl.when(p_id == 0)
    def _():
        c_ref[...] = jnp.zeros_like(c_ref)
        acc_ref[...] = jnp.zeros_like(acc_ref)
        i = _iota((LANES, PAGE), 0) % 16
        j = _iota((LANES, PAGE), 1)
        kv = new_ref[...]
        update([_dot_nt(q, kv[:, :width].astype(BF16))],
               [lambda a: _dot(a, kv[:, width:].astype(BF16))], (j < i) & (j < 8))

    update([_dot(q, pg[0].astype(BF16)) for pg in pages],
           [lambda a, pg=pg: _dot_nt(a, pg[1].astype(BF16)) for pg in pages], None)

    @pl.when(p_id == pl.num_programs(1) - 1)
    def _():
        acc = acc_ref[...]
        col_head = _iota((16, width), 1) // D_B
        o = jnp.zeros((16, width), F32)
        for h in range(H_B):
            o = o + jnp.where(col_head == h, acc[h * 16:(h + 1) * 16, :], 0.0)
        o_ref[...] = o


def _dec_fox_body(pt_ref, q_ref, new_ref, lfnew_ref, *rest):
    pages = rest[:PAGES_PER_STEP]
    lfs = rest[PAGES_PER_STEP:2 * PAGES_PER_STEP]
    o_ref, m_ref, l_ref, acc_ref, d_ref = rest[2 * PAGES_PER_STEP:]
    p_id = pl.program_id(1)
    q = q_ref[...]
    width = H_C * D_C
    rows = _iota((PAGE, PAGE), 0)
    cols = _iota((PAGE, PAGE), 1)
    later = (rows > cols).astype(BF16)
    upto = (rows <= cols).astype(BF16)
    expand = ((cols < 3 * H_C) & (cols % H_C == rows // 8)).astype(BF16)

    def head_table(lf_t, tri):
        parts = jnp.concatenate(_split3(lf_t), axis=0)
        d3 = _dot(parts, tri)
        return d3[0:H_C] + d3[H_C:2 * H_C] + d3[2 * H_C:3 * H_C]

    def to_rows(tab):
        parts = jnp.concatenate(_split3(tab) + (jnp.zeros((PAGE - 3 * H_C, PAGE), BF16),), axis=0)
        return _dot(expand, parts)

    @pl.when(p_id == 0)
    def _():
        m_ref[...] = jnp.full_like(m_ref, NEG)
        l_ref[...] = jnp.zeros_like(l_ref)
        acc_ref[...] = jnp.zeros_like(acc_ref)
        d_ref[...] = jnp.zeros_like(d_ref)
        i = rows % 8
        kv = new_ref[...]
        s = _dot_nt(q, kv[:, :width].astype(BF16)) - to_rows(head_table(lfnew_ref[...], upto))
        s = jnp.where((cols <= i) & (cols < 8), s, NEG)
        _softmax_update(s, lambda p: _dot(p, kv[:, width:].astype(BF16)), m_ref, l_ref, acc_ref)

    d_run = d_ref[...]
    scores = []
    for pg, lf in zip(pages, lfs):
        lf_t = lf[...]
        scores.append(_dot(q, pg[0].astype(BF16)) + to_rows(head_table(lf_t, later) + d_run))
        d_run = d_run + jnp.sum(lf_t, axis=1, keepdims=True)
    d_ref[...] = d_run

    def pv(p):
        out = _dot_nt(p[:, :PAGE], pages[0][1].astype(BF16))
        for j in range(1, PAGES_PER_STEP):
            out = out + _dot_nt(p[:, j * PAGE:(j + 1) * PAGE], pages[j][1].astype(BF16))
        return out

    _softmax_update(jnp.concatenate(scores, axis=1), pv, m_ref, l_ref, acc_ref)

    @pl.when(p_id == pl.num_programs(1) - 1)
    def _():
        o_all = acc_ref[...] / l_ref[...]
        col_head = _iota((8, width), 1) // D_C
        o = jnp.zeros((8, width), F32)
        for h in range(H_C):
            o = o + jnp.where(col_head == h, o_all[h * 8:(h + 1) * 8, :], 0.0)
        o_ref[...] = o


def _decode_attn(body, page_table, q_bd, new_kv, cache, layer, *, out_rows, out_cols, scratch,
                 reverse, extra=(), extra_specs=(), lf_new=None, lf_cache=None, name):
    nb, n_pages = page_table.shape
    steps = n_pages // PAGES_PER_STEP
    in_specs = list(extra_specs) + [
        pl.BlockSpec((None,) + q_bd.shape[1:], lambda b, p, pt: (b, 0, 0)),
        pl.BlockSpec((None,) + new_kv.shape[1:], lambda b, p, pt: (b, 0, 0))]
    args = list(extra) + [q_bd, new_kv]
    if lf_new is not None:
        in_specs.append(pl.BlockSpec((None,) + lf_new.shape[1:], lambda b, p, pt: (b, 0, 0)))
        args.append(lf_new)
    in_specs += _page_specs(cache.shape[2:], layer, n_pages, reverse)
    args += [cache] * PAGES_PER_STEP
    if lf_cache is not None:
        in_specs += _page_specs((H_C, PAGE), layer, n_pages, reverse)
        args += [lf_cache] * PAGES_PER_STEP
    grid_spec = pltpu.PrefetchScalarGridSpec(
        num_scalar_prefetch=1, grid=(nb, steps), in_specs=in_specs,
        out_specs=pl.BlockSpec((None, out_rows, out_cols), lambda b, p, pt: (b, 0, 0)),
        scratch_shapes=scratch)
    return pl.pallas_call(
        body, grid_spec=grid_spec,
        out_shape=jax.ShapeDtypeStruct((nb, out_rows, out_cols), F32),
        compiler_params=_params("parallel", "arbitrary"), name=name)(page_table, *args)


def _mem_body(q_ref, kv_ref, o_ref, kvb_ref, *, d_m):
    @pl.when(pl.program_id(1) == 0)
    def _():
        kvb_ref[...] = kv_ref[...].astype(BF16)

    hd = H_M * d_m
    for h in range(H_M):
        q = (q_ref[:, h * d_m:(h + 1) * d_m] * (d_m ** -0.5)).astype(BF16)
        s = _dot_nt(q, kvb_ref[:, h * d_m:(h + 1) * d_m])
        p = jnp.exp(s - jnp.max(s, axis=1, keepdims=True))
        o = _dot(p.astype(BF16), kvb_ref[:, hd + h * d_m: hd + (h + 1) * d_m])
        o_ref[:, h * d_m:(h + 1) * d_m] = (o / jnp.sum(p, axis=1, keepdims=True)).astype(o_ref.dtype)


def _mem_attn(q, mem_kv, kv_lead, *, nb, tq, name):
    M, D = q.shape
    nq = M // nb // tq
    n_mem = mem_kv.shape[-2]
    lead = tuple(kv_lead)
    return pl.pallas_call(
        functools.partial(_mem_body, d_m=D // H_M), grid=(nb, nq),
        in_specs=[pl.BlockSpec((tq, D), lambda b, i: (b * nq + i, 0)),
                  pl.BlockSpec((None,) * (len(lead) + 1) + (n_mem, 2 * D), lambda b, i: lead + (b, 0, 0))],
        out_specs=pl.BlockSpec((tq, D), lambda b, i: (b * nq + i, 0)),
        out_shape=jax.ShapeDtypeStruct((M, D), BF16),
        scratch_shapes=[pltpu.VMEM((n_mem, 2 * D), BF16)],
        compiler_params=_params("parallel", "arbitrary"), name=name)(q, mem_kv)


def _mm_ln_body(*refs, n_in, alpha):
    xs = refs[:n_in]
    ws = refs[n_in:2 * n_in]
    r_ref, g_ref, b_ref, o_ref = refs[2 * n_in:]
    y = _dot(xs[0][...].astype(BF16), ws[0][...])
    for x_ref, w_ref in zip(xs[1:], ws[1:]):
        y = y + _dot(x_ref[...].astype(BF16), w_ref[...])
    o_ref[...] = _layer_norm(alpha * r_ref[...] + y, g_ref[...], b_ref[...])


def _mm_ln(xs, w, layer, res, g, b, *, tm, alpha, name):
    M, D = res.shape
    in_specs, args, off = [], [], 0
    for x in xs:
        in_specs.append(pl.BlockSpec((tm, x.shape[1]), lambda m: (m, 0)))
    for x in xs:
        kx = x.shape[1]
        in_specs.append(pl.BlockSpec((None, kx, D), lambda m, o=off // kx: (layer, o, 0)))
        off += kx
    in_specs += [pl.BlockSpec((tm, D), lambda m: (m, 0)),
                 pl.BlockSpec((1, D), lambda m: (0, 0)), pl.BlockSpec((1, D), lambda m: (0, 0))]
    return pl.pallas_call(
        functools.partial(_mm_ln_body, n_in=len(xs), alpha=alpha), grid=(M // tm,),
        in_specs=in_specs, out_specs=pl.BlockSpec((tm, D), lambda m: (m, 0)),
        out_shape=jax.ShapeDtypeStruct((M, D), F32),
        compiler_params=_params("parallel"), name=name)(*xs, *([w] * len(xs)), res, g, b)


def _swiglu_chunk(xb, wg_ref, wu_ref, wd_ref):
    hg = _dot(xb, wg_ref[...])
    hu = _dot(xb, wu_ref[...])
    a = hg * jax.nn.sigmoid(hg) * hu
    return _dot(a.astype(BF16), wd_ref[...])


def _ffn_body(x_ref, wg_ref, wu_ref, wd_ref, g_ref, b_ref, o_ref, xb_ref, acc_ref, *, alpha):
    f = pl.program_id(1)

    @pl.when(f == 0)
    def _():
        xb_ref[...] = x_ref[...].astype(BF16)
        acc_ref[...] = jnp.zeros_like(acc_ref)

    acc_ref[...] += _swiglu_chunk(xb_ref[...], wg_ref, wu_ref, wd_ref)

    @pl.when(f == pl.num_programs(1) - 1)
    def _():
        o_ref[...] = _layer_norm(alpha * x_ref[...] + acc_ref[...], g_ref[...], b_ref[...])


def _ffn(x, w_gu, w_down, layer, g, b, *, tm, tf, alpha, name):
    M, D = x.shape
    nf = w_down.shape[1] // tf
    return pl.pallas_call(
        functools.partial(_ffn_body, alpha=alpha), grid=(M // tm, nf),
        in_specs=[pl.BlockSpec((tm, D), lambda m, f: (m, 0)),
                  pl.BlockSpec((None, D, tf), lambda m, f: (layer, 0, f)),
                  pl.BlockSpec((None, D, tf), lambda m, f: (layer, 0, nf + f)),
                  pl.BlockSpec((None, tf, D), lambda m, f: (layer, f, 0)),
                  pl.BlockSpec((1, D), lambda m, f: (0, 0)), pl.BlockSpec((1, D), lambda m, f: (0, 0))],
        out_specs=pl.BlockSpec((tm, D), lambda m, f: (m, 0)),
        out_shape=jax.ShapeDtypeStruct((M, D), F32),
        scratch_shapes=[pltpu.VMEM((tm, D), BF16), pltpu.VMEM((tm, D), F32)],
        compiler_params=_params("parallel", "arbitrary"), name=name)(x, w_gu, w_gu, w_down, g, b)


def _moe_body(x_ref, wr_ref, wg_ref, wu_ref, wd_ref, g_ref, b_ref, o_ref,
              comb_ref, rank_ref, comb_t_ref, rank_t_ref, xc_ref, yc_ref, acc_ref, cnt_ref, *, alpha):
    e = pl.program_id(1)
    f = pl.program_id(2)
    last_f = f == pl.num_programs(2) - 1
    T = x_ref.shape[0]

    @pl.when((e == 0) & (f == 0))
    def _():
        x = x_ref[...]
        acc_ref[...] = jnp.zeros_like(acc_ref)
        logits = jnp.dot(x, wr_ref[...], precision=lax.Precision.HIGHEST, preferred_element_type=F32)
        lane = _iota(logits.shape, 1).astype(F32)
        logits = jnp.where(lane < N_EXPERTS, logits, NEG)
        m1 = jnp.max(logits, axis=1, keepdims=True)
        i1 = jnp.min(jnp.where(logits == m1, lane, float(LANES)), axis=1, keepdims=True)
        rest = jnp.where(lane == i1, NEG, logits)
        m2 = jnp.max(rest, axis=1, keepdims=True)
        i2 = jnp.min(jnp.where(rest == m2, lane, float(LANES)), axis=1, keepdims=True)
        e2 = jnp.exp(m2 - m1)
        den = 1.0 + e2
        comb = jnp.where(lane == i1, 1.0 / den, 0.0) + jnp.where(lane == i2, e2 / den, 0.0)
        comb_ref[...] = comb
        comb_t = comb.T
        comb_t_ref[...] = comb_t
        r = _iota((T, T), 0)
        c = _iota((T, T), 1)
        rank_ref[...] = _dot((c < r).astype(BF16), (comb > 0.0).astype(BF16))
        rank_t_ref[...] = _dot((comb_t > 0.0).astype(BF16), (r < c).astype(BF16))

    sub = _iota((LANES, T), 0)

    @pl.when(f == 0)
    def _():
        comb_t = comb_t_ref[...]
        routed_t = jnp.sum(jnp.where((sub == e) & (comb_t > 0.0), 1.0, 0.0), axis=0, keepdims=True)
        rank_t = jnp.sum(jnp.where(sub == e, rank_t_ref[...], 0.0), axis=0, keepdims=True)
        n_rows = jnp.sum(routed_t).astype(jnp.int32)
        cnt_ref[0] = (n_rows + MOE_CHUNK - 1) // MOE_CHUNK
        xb = x_ref[...].astype(BF16)
        slot = _iota((MOE_CHUNK, T), 0).astype(F32)

        def compact(ci, carry):
            base = pl.multiple_of(ci * MOE_CHUNK, MOE_CHUNK)
            pick = ((rank_t == slot + base.astype(F32)) & (routed_t > 0.0)).astype(BF16)
            xc_ref[pl.ds(base, MOE_CHUNK), :] = _dot(pick, xb).astype(BF16)
            yc_ref[pl.ds(base, MOE_CHUNK), :] = jnp.zeros((MOE_CHUNK, yc_ref.shape[1]), F32)
            return carry

        lax.fori_loop(0, cnt_ref[0], compact, 0)

    def expert(ci, carry):
        base = pl.multiple_of(ci * MOE_CHUNK, MOE_CHUNK)
        yc_ref[pl.ds(base, MOE_CHUNK), :] += _swiglu_chunk(xc_ref[pl.ds(base, MOE_CHUNK), :], wg_ref, wu_ref, wd_ref)
        return carry

    lax.fori_loop(0, cnt_ref[0], expert, 0)

    @pl.when(last_f)
    def _():
        comb = comb_ref[...]
        lane = _iota(comb.shape, 1)
        gate = jnp.sum(jnp.where(lane == e, comb, 0.0), axis=1, keepdims=True)
        rank = jnp.sum(jnp.where(lane == e, rank_ref[...], 0.0), axis=1, keepdims=True)
        slot = _iota((T, MOE_CHUNK), 1).astype(F32)

        def scatter(ci, carry):
            base = pl.multiple_of(ci * MOE_CHUNK, MOE_CHUNK)
            place = ((rank == slot + base.astype(F32)) & (gate > 0.0)).astype(BF16)
            acc_ref[...] += gate * _dot(place, yc_ref[pl.ds(base, MOE_CHUNK), :].astype(BF16))
            return carry

        lax.fori_loop(0, cnt_ref[0], scatter, 0)

    @pl.when(last_f & (e == pl.num_programs(1) - 1))
    def _():
        o_ref[...] = _layer_norm(alpha * x_ref[...] + acc_ref[...], g_ref[...], b_ref[...])


def _moe(x, w_router, w_gu, w_down, layer, g, b, *, tm, tf, alpha, name):
    M, D = x.shape
    nf = w_down.shape[2] // tf
    assert tm % MOE_CHUNK == 0 and M % tm == 0
    once = pl.Buffered(1)
    return pl.pallas_call(
        functools.partial(_moe_body, alpha=alpha), grid=(M // tm, N_EXPERTS, nf),
        in_specs=[pl.BlockSpec((tm, D), lambda m, e, f: (m, 0), pipeline_mode=once),
                  pl.BlockSpec((None, D, LANES), lambda m, e, f: (layer, 0, 0), pipeline_mode=once),
                  pl.BlockSpec((None, None, D, tf), lambda m, e, f: (layer, e, 0, f)),
                  pl.BlockSpec((None, None, D, tf), lambda m, e, f: (layer, e, 0, nf + f)),
                  pl.BlockSpec((None, None, tf, D), lambda m, e, f: (layer, e, f, 0)),
                  pl.BlockSpec((1, D), lambda m, e, f: (0, 0)), pl.BlockSpec((1, D), lambda m, e, f: (0, 0))],
        out_specs=pl.BlockSpec((tm, D), lambda m, e, f: (m, 0)),
        out_shape=jax.ShapeDtypeStruct((M, D), F32),
        scratch_shapes=[pltpu.VMEM((tm, LANES), F32), pltpu.VMEM((tm, LANES), F32),
                        pltpu.VMEM((LANES, tm), F32), pltpu.VMEM((LANES, tm), F32),
                        pltpu.VMEM((tm, D), BF16), pltpu.VMEM((tm, D), F32), pltpu.VMEM((tm, D), F32),
                        pltpu.SMEM((1,), jnp.int32)],
        compiler_params=_params("parallel", "arbitrary", "arbitrary"), name=name)(
            x, w_router, w_gu, w_gu, w_down, g, b)


def _rope_tables(pos):
    half = DK_A // 2
    inv = 1.0 / (ROPE_THETA ** (jnp.arange(half, dtype=F32) / half))
    ang = pos.astype(F32)[:, None] * inv[None, :]
    cos, sin = jnp.cos(ang), jnp.sin(ang)
    return jnp.concatenate([cos] * 4, axis=1), jnp.concatenate([-sin, sin, -sin, sin], axis=1)


def _block_diag_queries(q, nb, n_heads, n_maps, d, rows_per_head, scale):
    g = n_heads * n_maps
    lq = q.shape[0] // nb
    q5 = (q * scale).reshape(nb, lq, g, d)
    q5 = jnp.pad(q5, ((0, 0), (0, rows_per_head - lq), (0, 0), (0, 0)))
    eye = jnp.eye(g, dtype=q.dtype)
    out = jnp.einsum('bigd,gx->bgixd', q5, eye)
    return out.reshape(nb, g * rows_per_head, g * d).astype(BF16)


def _pad_rows(x, nb, rows):
    lq = x.shape[0] // nb
    return jnp.pad(x.reshape(nb, lq, x.shape[1]), ((0, 0), (0, rows - lq), (0, 0)))


def kernel(x_prompt, x_sample, cache_a_kv, cache_b_kv, cache_c_kv, cache_c_logf, cache_mem_kv,
           page_table, mem_prompt, even_w_in, even_w_out, diff_lambda, diff_subln_g,
           odd_w_in, odd_b_f, odd_w_out, mem_w_q, mem_w_kv, mem_w_o, ffn_w_gu, ffn_w_down,
           moe_w_router, moe_w_gu, moe_w_down, ln_g, ln_b):
    depth = ln_g.shape[0]
    alpha = (2 * depth) ** 0.25
    nbp, L, D = x_prompt.shape
    nbs, Ls, _ = x_sample.shape
    n_pages = page_table.shape[1]
    n_pool = cache_a_kv.shape[1]
    past_len = n_pages * PAGE
    n_mem = mem_prompt.shape[1]
    assert Ls == 8 and n_pages % PAGES_PER_STEP == 0
    qa_w = H_A * 2 * DK_A
    qb_w = H_B * D_B
    qc_w = H_C * D_C
    even_in = 3 * qa_w + 3 * qb_w
    odd_main = 3 * qc_w

    w_even_in = even_w_in.astype(BF16)
    w_even_out = even_w_out.astype(BF16)
    w_odd_in = odd_w_in[:, :, :odd_main].astype(BF16)
    w_odd_f = jnp.pad(odd_w_in[:, :, odd_main:], ((0, 0), (0, 0), (0, LANES - H_C))).astype(BF16)
    b_odd_f = jnp.pad(odd_b_f, ((0, 0), (0, LANES - H_C)))[:, None, :]
    w_odd_out = odd_w_out.astype(BF16)
    w_mem_q = mem_w_q.astype(BF16)
    w_mem_kv = mem_w_kv.astype(BF16)
    w_mem_o = mem_w_o.astype(BF16)
    w_ffn_gu = ffn_w_gu.astype(BF16)
    w_ffn_down = ffn_w_down.astype(BF16)
    w_moe_gu = moe_w_gu.astype(BF16)
    w_moe_down = moe_w_down.astype(BF16)
    w_router = jnp.pad(moe_w_router, ((0, 0), (0, 0), (0, LANES - N_EXPERTS)))

    ca = cache_a_kv.reshape(cache_a_kv.shape[0], n_pool, PAGE * 2 * H_A, 2 * DK_A)
    cb = jnp.transpose(cache_b_kv, (0, 1, 3, 4, 5, 2)).reshape(cache_b_kv.shape[0], n_pool, 2, qb_w, PAGE)
    cc = jnp.transpose(cache_c_kv, (0, 1, 3, 4, 5, 2)).reshape(cache_c_kv.shape[0], n_pool, 2, qc_w, PAGE)
    clf = jnp.swapaxes(cache_c_logf, 2, 3)

    rope_p = _rope_tables(jnp.arange(L))
    rope_s = _rope_tables(past_len + (jnp.arange(nbs * Ls) % Ls))

    xp = x_prompt.reshape(nbp * L, D)
    xs = x_sample.reshape(nbs * Ls, D)
    mp = mem_prompt.reshape(nbp * n_mem, D)
    Mp, Ms = xp.shape[0], xs.shape[0]
    tmp, tms = min(512, Mp), Ms
    tq = min(256, L)
    tk = 2 * tq if (L // tq) % 2 == 0 else tq

    a_p, a_s, b_p, b_s, c_p, c_s, lf_p, lf_s, mem_rows = [], [], [], [], [], [], [], [], []
    for l in range(depth):
        i = l // 2
        g3 = ln_g[l][:, None, :]
        b3 = ln_b[l][:, None, :]
        mem_kv_p = _proj(mp, w_mem_kv, l, 2 * D, tm=min(512, mp.shape[0]), name=f"memkv{l}")
        mem_rows.append(mem_kv_p.reshape(nbp, n_mem, 2, H_M, D // H_M))
        mem_kv_p = mem_kv_p.reshape(nbp, n_mem, 2 * D)
        if l % 2 == 0:
            lam_init = _lambda_init(l)
            lv = diff_lambda[i]
            sg = diff_subln_g[i][None, :]
            hp = _proj(xp, w_even_in, i, even_in, tm=tmp, rope=rope_p, n_rope=2, name=f"even_in_p{l}")
            a_p.append(hp[:, qa_w:3 * qa_w].reshape(nbp, L, 2, H_A, 2 * DK_A))
            b_p.append(hp[:, 3 * qa_w + qb_w:].reshape(nbp, L, 2, H_B, D_B))
            oa = _prompt_attn(functools.partial(_diff_body, tq=tq, tk=tk, lam_init=lam_init), hp, nb=nbp, L=L,
                              n_groups=H_A, qcol=0, kcol=H_A, vcol=2 * H_A, tq=tq, scratch=(), extra=(lv, sg),
                              extra_specs=(pl.BlockSpec(lv.shape, lambda b, g, i: (0, 0)),
                                           pl.BlockSpec(sg.shape, lambda b, g, i: (0, 0))),
                              name=f"diff_p{l}")
            ob = _prompt_attn(functools.partial(_sb_body, tq=tq), hp, nb=nbp, L=L, n_groups=H_B // 2,
                              qcol=3 * H_A, kcol=3 * H_A + H_B // 2, vcol=3 * H_A + H_B, tq=tq, scratch=(),
                              name=f"stick_p{l}")
            xp = _mm_ln([oa, ob], w_even_out, i, xp, g3[0], b3[0], tm=tmp, alpha=alpha, name=f"even_out_p{l}")
            hs = _proj(xs, w_even_in, i, even_in, tm=tms, rope=rope_s, n_rope=2, name=f"even_in_s{l}")
            a_new = hs[:, qa_w:3 * qa_w]
            b_new = hs[:, 3 * qa_w + qb_w:]
            a_s.append(a_new.reshape(nbs, Ls, 2, H_A, 2 * DK_A))
            b_s.append(b_new.reshape(nbs, Ls, 2, H_B, D_B))
            qa_bd = _block_diag_queries(hs[:, :qa_w], nbs, H_A, 2, DK_A, 16, DK_A ** -0.5)
            qb_bd = _block_diag_queries(hs[:, 3 * qa_w:3 * qa_w + qb_w], nbs, H_B, 1, D_B, 16, D_B ** -0.5)
            oa = _decode_attn(functools.partial(_dec_diff_body, lam_init=lam_init), page_table, qa_bd,
                              _pad_rows(a_new, nbs, PAGE), ca, i, out_rows=16, out_cols=qa_w,
                              scratch=[pltpu.VMEM((LANES, 1), F32), pltpu.VMEM((LANES, 1), F32),
                                       pltpu.VMEM((LANES, qa_w), F32)],
                              reverse=False, extra=(lv, sg),
                              extra_specs=(pl.BlockSpec(lv.shape, lambda b, p, pt: (0, 0)),
                                           pl.BlockSpec(sg.shape, lambda b, p, pt: (0, 0))),
                              name=f"diff_s{l}")
            ob = _decode_attn(_dec_sb_body, page_table, qb_bd, _pad_rows(b_new, nbs, PAGE), cb, i,
                              out_rows=16, out_cols=qb_w,
                              scratch=[pltpu.VMEM((LANES, 1), F32), pltpu.VMEM((LANES, qb_w), F32)],
                              reverse=True, name=f"stick_s{l}")
            oa = oa[:, :Ls].reshape(Ms, qa_w)
            ob = ob[:, :Ls].reshape(Ms, qb_w)
            xs = _mm_ln([oa, ob], w_even_out, i, xs, g3[0], b3[0], tm=tms, alpha=alpha, name=f"even_out_s{l}")
        else:
            hp = _proj(xp, w_odd_in, i, odd_main, tm=tmp, name=f"odd_in_p{l}")
            c_p.append(hp[:, qc_w:].reshape(nbp, L, 2, H_C, D_C))
            lf, cs = _logf(xp, w_odd_f[i], b_odd_f[i], nb=nbp, tl=tq, with_cumsum=True, name=f"logf_p{l}")
            lf_p.append(lf[:, :H_C].reshape(nbp, L, H_C))
            nq = L // tq
            oc = _prompt_attn(functools.partial(_fox_body, tq=tq, tk=tk), hp, nb=nbp, L=L, n_groups=H_C // 2,
                              qcol=0, kcol=H_C // 2, vcol=H_C, tq=tq, scratch=(2,), extra=(cs, cs),
                              extra_specs=(pl.BlockSpec((tq, LANES), lambda b, g, i: (b * nq + i, 0)),
                                           pl.BlockSpec((L, LANES), lambda b, g, i: (b, 0))),
                              name=f"fox_p{l}")
            xp = _mm_ln([oc], w_odd_out, i, xp, g3[0], b3[0], tm=tmp, alpha=alpha, name=f"odd_out_p{l}")
            hs = _proj(xs, w_odd_in, i, odd_main, tm=tms, name=f"odd_in_s{l}")
            c_new = hs[:, qc_w:]
            c_s.append(c_new.reshape(nbs, Ls, 2, H_C, D_C))
            lf, _ = _logf(xs, w_odd_f[i], b_odd_f[i], nb=1, tl=Ms, with_cumsum=False, name=f"logf_s{l}")
            lf_s.append(lf[:, :H_C].reshape(nbs, Ls, H_C))
            lf_new_t = jnp.swapaxes(_pad_rows(lf[:, :H_C], nbs, PAGE), 1, 2)
            qc_bd = _block_diag_queries(hs[:, :qc_w], nbs, H_C, 1, D_C, 8, D_C ** -0.5)
            oc = _decode_attn(_dec_fox_body, page_table, qc_bd, _pad_rows(c_new, nbs, PAGE), cc, i,
                              out_rows=8, out_cols=qc_w,
                              scratch=[pltpu.VMEM((LANES, 1), F32), pltpu.VMEM((LANES, 1), F32),
                                       pltpu.VMEM((LANES, qc_w), F32), pltpu.VMEM((H_C, 1), F32)],
                              reverse=True, lf_new=lf_new_t, lf_cache=clf, name=f"fox_s{l}")
            oc = oc.reshape(Ms, qc_w)
            xs = _mm_ln([oc], w_odd_out, i, xs, g3[0], b3[0], tm=tms, alpha=alpha, name=f"odd_out_s{l}")

        qm = _proj(xp, w_mem_q, l, D, tm=tmp, name=f"mem_q_p{l}")
        om = _mem_attn(qm, mem_kv_p, (), nb=nbp, tq=tq, name=f"mem_attn_p{l}")
        xp = _mm_ln([om], w_mem_o, l, xp, g3[1], b3[1], tm=tmp, alpha=alpha, name=f"mem_out_p{l}")
        qm = _proj(xs, w_mem_q, l, D, tm=tms, name=f"mem_q_s{l}")
        om = _mem_attn(qm, cache_mem_kv.reshape(depth, nbs, n_mem, 2 * D), (l,), nb=nbs, tq=Ls,
                       name=f"mem_attn_s{l}")
        xs = _mm_ln([om], w_mem_o, l, xs, g3[1], b3[1], tm=tms, alpha=alpha, name=f"mem_out_s{l}")

        if l % 2 == 0:
            xp = _ffn(xp, w_ffn_gu, w_ffn_down, i, g3[2], b3[2], tm=min(1024, Mp), tf=256, alpha=alpha, name=f"ffn_p{l}")
            xs = _ffn(xs, w_ffn_gu, w_ffn_down, i, g3[2], b3[2], tm=tms, tf=256, alpha=alpha, name=f"ffn_s{l}")
        else:
            tf_moe = w_moe_down.shape[2] // 2
            xp = _moe(xp, w_router, w_moe_gu, w_moe_down, i, g3[2], b3[2], tm=min(1024, Mp), tf=tf_moe,
                      alpha=alpha, name=f"moe_p{l}")
            xs = _moe(xs, w_router, w_moe_gu, w_moe_down, i, g3[2], b3[2], tm=tms, tf=tf_moe, alpha=alpha,
                      name=f"moe_s{l}")

    return (xp.reshape(nbp, L, D), xs.reshape(nbs, Ls, D),
            jnp.stack(a_p), jnp.stack(a_s), jnp.stack(b_p), jnp.stack(b_s),
            jnp.stack(c_p), jnp.stack(c_s), jnp.stack(lf_p), jnp.stack(lf_s), jnp.stack(mem_rows))
```

```python
import functools
import math

import jax
import jax.numpy as jnp
from jax import lax
from jax.experimental import pallas as pl
from jax.experimental.pallas import tpu as pltpu

F32 = jnp.float32
BF16 = jnp.bfloat16

H_A, DK_A = 4, 64
H_B, D_B = 8, 64
H_C, D_C = 16, 64
H_M = 4
N_EXPERTS, TOP_K = 8, 2
ROPE_THETA = 10000.0
LN_EPS = 1e-5
RMS_EPS = 1e-5
PAGE = 128
LANES = 128
NEG = -1e30
SB_CUTOFF = -104.0
VMEM_LIMIT = 56 << 20
PAGES_PER_STEP = 8
MOE_CHUNK = 128


def _lambda_init(layer_idx):
    return 0.8 - 0.6 * math.exp(-0.3 * layer_idx)


def _params(*sem):
    return pltpu.CompilerParams(dimension_semantics=sem, vmem_limit_bytes=VMEM_LIMIT)


def _dot(a, b):
    return jnp.dot(a, b, preferred_element_type=F32)


def _dot_nt(a, b):
    return lax.dot_general(a, b, (((1,), (1,)), ((), ())), preferred_element_type=F32)


def _log_sigmoid(z):
    return jnp.minimum(z, 0.0) - jnp.log1p(jnp.exp(-jnp.abs(z)))


def _layer_norm(v, g, b):
    mu = jnp.mean(v, axis=1, keepdims=True)
    d = v - mu
    var = jnp.mean(d * d, axis=1, keepdims=True)
    return d * lax.rsqrt(var + LN_EPS) * g + b


def _split2(x):
    hi = x.astype(BF16)
    lo = (x - hi.astype(F32)).astype(BF16)
    return hi, lo


def _split3(x):
    h1 = x.astype(BF16)
    r1 = x - h1.astype(F32)
    h2 = r1.astype(BF16)
    h3 = (r1 - h2.astype(F32)).astype(BF16)
    return h1, h2, h3


def _iota(shape, dim):
    return lax.broadcasted_iota(jnp.int32, shape, dim)


def _proj_body(*refs, n_rope):
    if n_rope:
        x_ref, w_ref, cos_ref, sin_ref, o_ref = refs
    else:
        x_ref, w_ref, o_ref = refs
    h = _dot(x_ref[...].astype(BF16), w_ref[...])
    if not n_rope:
        o_ref[...] = h
        return
    n = pl.program_id(1)
    tn = h.shape[1]

    @pl.when(n < n_rope)
    def _():
        reps = tn // LANES
        c = jnp.concatenate([cos_ref[...]] * reps, axis=1)
        s = jnp.concatenate([sin_ref[...]] * reps, axis=1)
        lane = _iota(h.shape, 1)
        first = (lane % 64) < 32
        rot = jnp.where(first, pltpu.roll(h, tn - 32, 1), pltpu.roll(h, 32, 1))
        o_ref[...] = h * c + rot * s

    @pl.when(n >= n_rope)
    def _():
        o_ref[...] = h


def _proj(x, w, layer, n_cols, *, tm, rope=None, rope_cols=0, name):
    M, K = x.shape
    tn = 1024 if n_cols % 1024 == 0 and rope_cols % 1024 == 0 else 512
    n_rope = rope_cols // tn
    grid = (M // tm, n_cols // tn)
    in_specs = [pl.BlockSpec((tm, K), lambda m, n: (m, 0)),
                pl.BlockSpec((None, K, tn), lambda m, n: (layer, 0, n))]
    args = [x, w]
    if n_rope:
        cos, sin = rope
        nt = cos.shape[0] // tm
        in_specs += [pl.BlockSpec((tm, LANES), lambda m, n: (m % nt, 0)),
                     pl.BlockSpec((tm, LANES), lambda m, n: (m % nt, 0))]
        args += [cos, sin]
    return pl.pallas_call(
        functools.partial(_proj_body, n_rope=n_rope),
        grid=grid, in_specs=in_specs,
        out_specs=pl.BlockSpec((tm, tn), lambda m, n: (m, n)),
        out_shape=jax.ShapeDtypeStruct((M, n_cols), F32),
        compiler_params=_params("parallel", "arbitrary"), name=name)(*args)


def _logf_body(x_ref, w_ref, b_ref, lf_ref, cs_ref, carry_ref, *, with_cumsum):
    t = pl.program_id(1)

    @pl.when(t == 0)
    def _():
        carry_ref[...] = jnp.zeros_like(carry_ref)

    f = _dot(x_ref[...].astype(BF16), w_ref[...]) + b_ref[...]
    lf = _log_sigmoid(f)
    lf_ref[...] = lf
    tl = lf.shape[0]
    if not with_cumsum:
        cs_ref[...] = lf
        return
    tri = (_iota((tl, tl), 1) <= _iota((tl, tl), 0)).astype(BF16)
    parts = _split3(lf)
    cs = carry_ref[...] + _dot(tri, parts[0]) + _dot(tri, parts[1]) + _dot(tri, parts[2])
    cs_ref[...] = cs
    carry_ref[...] = cs[tl - 1:tl, :]


def _logf(x, wf, bf, *, nb, tl, with_cumsum, name):
    M, K = x.shape
    nt = M // nb // tl
    return pl.pallas_call(
        functools.partial(_logf_body, with_cumsum=with_cumsum), grid=(nb, nt),
        in_specs=[pl.BlockSpec((tl, K), lambda b, t: (b * nt + t, 0)),
                  pl.BlockSpec((K, LANES), lambda b, t: (0, 0)),
                  pl.BlockSpec((1, LANES), lambda b, t: (0, 0))],
        out_specs=[pl.BlockSpec((tl, LANES), lambda b, t: (b * nt + t, 0)),
                   pl.BlockSpec((tl, LANES), lambda b, t: (b * nt + t, 0))],
        out_shape=[jax.ShapeDtypeStruct((M, LANES), F32)] * 2,
        scratch_shapes=[pltpu.VMEM((1, LANES), F32)],
        compiler_params=_params("parallel", "arbitrary"), name=name)(x, wf, bf)


def _load_kv(k_ref, v_ref, kb_ref, vb_ref):
    @pl.when(pl.program_id(2) == 0)
    def _():
        kb_ref[...] = k_ref[...].astype(BF16)
        vb_ref[...] = v_ref[...].astype(BF16)


def _softmax_step(s, v, m, l, acc):
    m_new = jnp.maximum(m, jnp.max(s, axis=1, keepdims=True))
    alpha = jnp.exp(m - m_new)
    p = jnp.exp(s - m_new)
    l = alpha * l + jnp.sum(p, axis=1, keepdims=True)
    acc = alpha * acc + _dot(p.astype(BF16), v)
    return m_new, l, acc


def _key_blocks(qi, tq, tk):
    return ((qi + 1) * tq + tk - 1) // tk


def _causal_mask(qi, kb, tq, tk):
    return kb * tk + _iota((tq, tk), 1) <= qi * tq + _iota((tq, tk), 0)


def _diff_body(lv_ref, g_ref, q_ref, k_ref, v_ref, o_ref, kb_ref, vb_ref, *, tq, tk, lam_init):
    _load_kv(k_ref, v_ref, kb_ref, vb_ref)
    qi = pl.program_id(2)
    q = q_ref[...] * (DK_A ** -0.5)
    lane = _iota(q.shape, 1)
    qs = (jnp.where(lane < 64, q, 0.0).astype(BF16), jnp.where(lane >= 64, q, 0.0).astype(BF16))

    def step(kb, carry, masked):
        off = pl.multiple_of(kb * tk, tk)
        k = kb_ref[pl.ds(off, tk), :]
        v = vb_ref[pl.ds(off, tk), :]
        out = []
        for c in range(2):
            s = _dot_nt(qs[c], k)
            if masked:
                s = jnp.where(_causal_mask(qi, kb, tq, tk), s, NEG)
            out.append(_softmax_step(s, v, *carry[c]))
        return tuple(out)

    z1 = jnp.zeros((tq, 1), F32)
    init = ((z1 + NEG, z1, jnp.zeros((tq, LANES), F32)),) * 2
    nk = _key_blocks(qi, tq, tk)
    carry = lax.fori_loop(0, nk - 1, lambda i, c: step(i, c, False), init)
    (m0, l0, a0), (m1, l1, a1) = step(nk - 1, carry, True)
    lv = lv_ref[...]
    lam = (jnp.exp(jnp.sum(lv[0:1, :] * lv[1:2, :], axis=1, keepdims=True))
           - jnp.exp(jnp.sum(lv[2:3, :] * lv[3:4, :], axis=1, keepdims=True)) + lam_init)
    o = a0 / l0 - lam * (a1 / l1)
    o = o * lax.rsqrt(jnp.mean(o * o, axis=1, keepdims=True) + RMS_EPS) * g_ref[...]
    o_ref[...] = (o * (1.0 - lam_init)).astype(o_ref.dtype)


def _sb_body(q_ref, k_ref, v_ref, o_ref, kb_ref, vb_ref, *, tq):
    _load_kv(k_ref, v_ref, kb_ref, vb_ref)
    qi = pl.program_id(2)
    q = q_ref[...] * (D_B ** -0.5)
    lane = _iota(q.shape, 1)
    rows = _iota((tq, tq), 0)
    cols = _iota((tq, tq), 1)
    later = (rows > cols).astype(BF16)
    strict = cols < rows
    qs = [jnp.where((lane >= 64 * hh) & (lane < 64 * hh + 64), q, 0.0).astype(BF16) for hh in range(2)]

    def step(ki, carry, diag):
        off = pl.multiple_of(ki * tq, tq)
        k = kb_ref[pl.ds(off, tq), :]
        v = vb_ref[pl.ds(off, tq), :]
        out = []
        for hh in range(2):
            c_run, acc = carry[hh]
            z = _dot_nt(qs[hh], k)
            ls = _log_sigmoid(z)
            lom = ls - z
            if diag:
                lom = jnp.where(strict, lom, 0.0)
            hi, lo = _split2(lom)
            suffix = _dot(hi, later) + _dot(lo, later)
            a = jnp.exp(ls + suffix + c_run)
            if diag:
                a = jnp.where(strict, a, 0.0)
            out.append((c_run + jnp.sum(lom, axis=1, keepdims=True), acc + _dot(a.astype(BF16), v)))
        return tuple(out)

    def live(carry):
        return jnp.max(jnp.maximum(carry[0][0], carry[1][0])) > SB_CUTOFF

    init = ((jnp.zeros((tq, 1), F32), jnp.zeros((tq, LANES), F32)),) * 2
    carry = step(qi, init, True)
    _, carry = lax.while_loop(lambda st: (st[0] < qi) & live(st[1]),
                              lambda st: (st[0] + 1, step(qi - 1 - st[0], st[1], False)),
                              (jnp.int32(0), carry))
    o_ref[...] = jnp.where(lane < 64, carry[0][1], carry[1][1]).astype(o_ref.dtype)


def _fox_extra_lanes(lane, hh, parts, ones_first):
    e0 = 64 * (1 - hh)
    one = jnp.where((lane >= e0 + (0 if ones_first else 3)) & (lane < e0 + (3 if ones_first else 6)), 1.0, 0.0)
    p0 = e0 + (3 if ones_first else 0)
    return (one + jnp.where(lane == p0, parts[0], 0.0) + jnp.where(lane == p0 + 1, parts[1], 0.0)
            + jnp.where(lane == p0 + 2, parts[2], 0.0))


def _split3_f32(x):
    h1 = x.astype(BF16).astype(F32)
    r1 = x - h1
    h2 = r1.astype(BF16).astype(F32)
    return h1, h2, (r1 - h2).astype(BF16).astype(F32)


def _fox_body(fq_ref, fk_ref, q_ref, k_ref, v_ref, o_ref, ka_ref, va_ref, *, tq, tk):
    g = pl.program_id(1)
    qi = pl.program_id(2)

    @pl.when(qi == 0)
    def _():
        k = k_ref[...]
        v = v_ref[...]
        f_all = fk_ref[...]
        lane = _iota(k.shape, 1)
        for hh in range(2):
            own = (lane >= 64 * hh) & (lane < 64 * hh + 64)
            f = jnp.sum(jnp.where(lane == 2 * g + hh, f_all, 0.0), axis=1, keepdims=True)
            parts = [-p for p in _split3_f32(f)]
            ka_ref[hh] = jnp.where(own, k, _fox_extra_lanes(lane, hh, parts, True)).astype(BF16)
            va_ref[hh] = jnp.where(own, v, jnp.where(lane == 64 * (1 - hh), 1.0, 0.0)).astype(BF16)

    q = q_ref[...] * (D_C ** -0.5)
    lane = _iota(q.shape, 1)
    fq_all = fq_ref[...]
    qs = []
    for hh in range(2):
        own = (lane >= 64 * hh) & (lane < 64 * hh + 64)
        fq = jnp.sum(jnp.where(lane == 2 * g + hh, fq_all, 0.0), axis=1, keepdims=True)
        qs.append(jnp.where(own, q, _fox_extra_lanes(lane, hh, _split3_f32(fq), False)).astype(BF16))

    def step(kb, carry, masked):
        off = pl.multiple_of(kb * tk, tk)
        out = []
        for hh in range(2):
            m, acc = carry[hh]
            s = _dot_nt(qs[hh], ka_ref[hh, pl.ds(off, tk), :])
            if masked:
                s = jnp.where(_causal_mask(qi, kb, tq, tk), s, NEG)
            m_new = jnp.maximum(m, jnp.max(s, axis=1, keepdims=True))
            p = jnp.exp(s - m_new).astype(BF16)
            out.append((m_new, jnp.exp(m - m_new) * acc + _dot(p, va_ref[hh, pl.ds(off, tk), :])))
        return tuple(out)

    init = ((jnp.full((tq, 1), NEG, F32), jnp.zeros((tq, LANES), F32)),) * 2
    nk = _key_blocks(qi, tq, tk)
    carry = lax.fori_loop(0, nk - 1, lambda i, c: step(i, c, False), init)
    (_, a0), (_, a1) = step(nk - 1, carry, True)
    o_ref[...] = jnp.where(lane < 64, a0 / a0[:, 64:65], a1 / a1[:, 0:1]).astype(o_ref.dtype)


def _prompt_attn(body, h_all, *, nb, L, n_groups, qcol, kcol, vcol, tq, scratch, extra=(), extra_specs=(), name):
    nq = L // tq
    in_specs = list(extra_specs) + [
        pl.BlockSpec((tq, LANES), lambda b, g, i: (b * nq + i, qcol + g)),
        pl.BlockSpec((L, LANES), lambda b, g, i: (b, kcol + g)),
        pl.BlockSpec((L, LANES), lambda b, g, i: (b, vcol + g))]
    return pl.pallas_call(
        body, grid=(nb, n_groups, nq), in_specs=in_specs,
        out_specs=pl.BlockSpec((tq, LANES), lambda b, g, i: (b * nq + i, g)),
        out_shape=jax.ShapeDtypeStruct((nb * L, n_groups * LANES), BF16),
        scratch_shapes=[pltpu.VMEM(scratch + (L, LANES), BF16), pltpu.VMEM(scratch + (L, LANES), BF16)],
        compiler_params=_params("parallel", "parallel", "arbitrary"), name=name)(*extra, h_all, h_all, h_all)


def _page_specs(block, layer, n_pages, reverse):
    specs = []
    for j in range(PAGES_PER_STEP):
        def idx(b, p, pt, j=j):
            lp = p * PAGES_PER_STEP + j
            if reverse:
                lp = n_pages - 1 - lp
            return (layer, pt[b, lp]) + (0,) * len(block)
        specs.append(pl.BlockSpec((None, None) + block, idx))
    return specs


def _softmax_update(s, pv, m_ref, l_ref, acc_ref):
    m = m_ref[...]
    m_new = jnp.maximum(m, jnp.max(s, axis=1, keepdims=True))
    alpha = jnp.exp(m - m_new)
    p = jnp.exp(s - m_new)
    m_ref[...] = m_new
    l_ref[...] = alpha * l_ref[...] + jnp.sum(p, axis=1, keepdims=True)
    acc_ref[...] = alpha * acc_ref[...] + pv(p.astype(BF16))


def _dec_diff_body(pt_ref, lv_ref, g_ref, q_ref, new_ref, *rest, lam_init):
    pages = rest[:PAGES_PER_STEP]
    o_ref, m_ref, l_ref, acc_ref = rest[PAGES_PER_STEP:]
    p_id = pl.program_id(1)
    q = q_ref[...]
    width = H_A * 2 * DK_A

    def update(ks, vs, mask):
        s = jnp.concatenate([_dot_nt(q, k.astype(BF16)) for k in ks], axis=1)
        if mask is not None:
            s = jnp.where(mask, s, NEG)

        def pv(p):
            out = _dot(p[:, :PAGE], vs[0].astype(BF16))
            for j in range(1, len(vs)):
                out = out + _dot(p[:, j * PAGE:(j + 1) * PAGE], vs[j].astype(BF16))
            return out

        _softmax_update(s, pv, m_ref, l_ref, acc_ref)

    @pl.when(p_id == 0)
    def _():
        m_ref[...] = jnp.full_like(m_ref, NEG)
        l_ref[...] = jnp.zeros_like(l_ref)
        acc_ref[...] = jnp.zeros_like(acc_ref)
        i = _iota((LANES, PAGE), 0) % 16
        j = _iota((LANES, PAGE), 1)
        kv = new_ref[...]
        update([kv[:, :width]], [kv[:, width:]], (j <= i) & (j < 8))

    def head_rows(pg, kv):
        return jnp.concatenate([pg[pl.ds(kv * H_A + h, PAGE, stride=2 * H_A), :] for h in range(H_A)], axis=1)

    update([head_rows(pg, 0) for pg in pages], [head_rows(pg, 1) for pg in pages], None)

    @pl.when(p_id == pl.num_programs(1) - 1)
    def _():
        lv = lv_ref[...]
        lam = (jnp.exp(jnp.sum(lv[0:1, :] * lv[1:2, :], axis=1, keepdims=True))
               - jnp.exp(jnp.sum(lv[2:3, :] * lv[3:4, :], axis=1, keepdims=True)) + lam_init)
        o_all = acc_ref[...] / l_ref[...]
        for h in range(H_A):
            r0 = (2 * h) * 16
            cs = slice(h * LANES, (h + 1) * LANES)
            o = o_all[r0:r0 + 16, cs] - lam * o_all[r0 + 16:r0 + 32, cs]
            o = o * lax.rsqrt(jnp.mean(o * o, axis=1, keepdims=True) + RMS_EPS) * g_ref[...]
            o_ref[:, cs] = o * (1.0 - lam_init)


def _dec_sb_body(pt_ref, lim_ref, q_ref, new_ref, c_in_ref, acc_in_ref, *rest, first):
    pages = rest[:PAGES_PER_STEP]
    o_ref, c_out_ref, acc_out_ref, c_ref, acc_ref = rest[PAGES_PER_STEP:]
    p_id = pl.program_id(1)
    q = q_ref[...]
    width = H_B * D_B
    later = (_iota((PAGE, PAGE), 0) > _iota((PAGE, PAGE), 1)).astype(BF16)

    def update(zs, pvs, mask):
        c_run = c_ref[...]
        total = None
        for z, pv in zip(zs, pvs):
            ls = _log_sigmoid(z)
            lom = ls - z
            if mask is not None:
                lom = jnp.where(mask, lom, 0.0)
            hi, lo = _split2(lom)
            suffix = _dot(hi, later) + _dot(lo, later)
            a = jnp.exp(ls + suffix + c_run)
            if mask is not None:
                a = jnp.where(mask, a, 0.0)
            out = pv(a.astype(BF16))
            total = out if total is None else total + out
            c_run = c_run + jnp.sum(lom, axis=1, keepdims=True)
        acc_ref[...] += total
        c_ref[...] = c_run

    @pl.when(p_id == 0)
    def _():
        if first:
            c_ref[...] = jnp.zeros_like(c_ref)
            acc_ref[...] = jnp.zeros_like(acc_ref)
            i = _iota((LANES, PAGE), 0) % 16
            j = _iota((LANES, PAGE), 1)
            kv = new_ref[...]
            update([_dot_nt(q, kv[:, :width].astype(BF16))],
                   [lambda a: _dot(a, kv[:, width:].astype(BF16))], (j < i) & (j < 8))
        else:
            c_ref[...] = c_in_ref[:, 0:1]
            acc_ref[...] = acc_in_ref[...]

    @pl.when(p_id < lim_ref[pl.program_id(0)])
    def _():
        update([_dot(q, pg[0].astype(BF16)) for pg in pages],
               [lambda a, pg=pg: _dot_nt(a, pg[1].astype(BF16)) for pg in pages], None)

    @pl.when(p_id == pl.num_programs(1) - 1)
    def _():
        acc = acc_ref[...]
        col_head = _iota((16, width), 1) // D_B
        o = jnp.zeros((16, width), F32)
        for h in range(H_B):
            o = o + jnp.where(col_head == h, acc[h * 16:(h + 1) * 16, :], 0.0)
        o_ref[...] = o
        c_out_ref[...] = jnp.broadcast_to(c_ref[...], c_out_ref.shape)
        acc_out_ref[...] = acc


def _decode_stick(page_table, q_bd, new_kv, cache, layer, *, name):
    nb, n_pages = page_table.shape
    rows, width = q_bd.shape[1], q_bd.shape[2]
    n_steps = n_pages // PAGES_PER_STEP

    def call(first, steps, step_off, lim, c_in, acc_in, call_name):
        def page_spec(j):
            def idx(b, p, pt, lim_ref):
                p_eff = jnp.minimum(p, jnp.maximum(lim_ref[b] - 1, 0))
                return (layer, pt[b, n_pages - 1 - ((step_off + p_eff) * PAGES_PER_STEP + j)], 0, 0, 0)
            return pl.BlockSpec((None, None) + cache.shape[2:], idx)

        per_seq = lambda a: pl.BlockSpec((None,) + a.shape[1:], lambda b, p, pt, lim_ref: (b, 0, 0))
        out_shapes = [jax.ShapeDtypeStruct((nb, 16, width), F32), jax.ShapeDtypeStruct((nb, rows, LANES), F32),
                      jax.ShapeDtypeStruct((nb, rows, width), F32)]
        grid_spec = pltpu.PrefetchScalarGridSpec(
            num_scalar_prefetch=2, grid=(nb, steps),
            in_specs=[per_seq(q_bd), per_seq(new_kv), per_seq(c_in), per_seq(acc_in)]
            + [page_spec(j) for j in range(PAGES_PER_STEP)],
            out_specs=[per_seq(s) for s in out_shapes],
            scratch_shapes=[pltpu.VMEM((rows, 1), F32), pltpu.VMEM((rows, width), F32)])
        return pl.pallas_call(
            functools.partial(_dec_sb_body, first=first), grid_spec=grid_spec, out_shape=out_shapes,
            compiler_params=_params("parallel", "arbitrary"), name=call_name)(
                page_table, lim, q_bd, new_kv, c_in, acc_in, *([cache] * PAGES_PER_STEP))

    o, c_run, acc = call(True, 1, 0, jnp.ones((nb,), jnp.int32),
                         jnp.zeros((nb, rows, LANES), F32), jnp.zeros((nb, rows, width), F32), name + "_head")
    if n_steps == 1:
        return o
    live = jnp.max(c_run[:, :, 0], axis=1) > SB_CUTOFF
    lim = jnp.where(live, n_steps - 1, 0).astype(jnp.int32)
    return call(False, n_steps - 1, 1, lim, c_run, acc, name + "_tail")[0]


def _dec_fox_body(pt_ref, q_ref, new_ref, lfnew_ref, *rest):
    pages = rest[:PAGES_PER_STEP]
    lfs = rest[PAGES_PER_STEP:2 * PAGES_PER_STEP]
    o_ref, m_ref, l_ref, acc_ref, d_ref = rest[2 * PAGES_PER_STEP:]
    p_id = pl.program_id(1)
    q = q_ref[...]
    width = H_C * D_C
    rows = _iota((PAGE, PAGE), 0)
    cols = _iota((PAGE, PAGE), 1)
    later = (rows > cols).astype(BF16)
    upto = (rows <= cols).astype(BF16)
    expand = ((cols < 3 * H_C) & (cols % H_C == rows // 8)).astype(BF16)

    def head_table(lf_t, tri):
        parts = jnp.concatenate(_split3(lf_t), axis=0)
        d3 = _dot(parts, tri)
        return d3[0:H_C] + d3[H_C:2 * H_C] + d3[2 * H_C:3 * H_C]

    def to_rows(tab):
        parts = jnp.concatenate(_split3(tab) + (jnp.zeros((PAGE - 3 * H_C, PAGE), BF16),), axis=0)
        return _dot(expand, parts)

    @pl.when(p_id == 0)
    def _():
        m_ref[...] = jnp.full_like(m_ref, NEG)
        l_ref[...] = jnp.zeros_like(l_ref)
        acc_ref[...] = jnp.zeros_like(acc_ref)
        d_ref[...] = jnp.zeros_like(d_ref)
        i = rows % 8
        kv = new_ref[...]
        s = _dot_nt(q, kv[:, :width].astype(BF16)) - to_rows(head_table(lfnew_ref[...], upto))
        s = jnp.where((cols <= i) & (cols < 8), s, NEG)
        _softmax_update(s, lambda p: _dot(p, kv[:, width:].astype(BF16)), m_ref, l_ref, acc_ref)

    d_run = d_ref[...]
    scores = []
    for pg, lf in zip(pages, lfs):
        lf_t = lf[...]
        scores.append(_dot(q, pg[0].astype(BF16)) + to_rows(head_table(lf_t, later) + d_run))
        d_run = d_run + jnp.sum(lf_t, axis=1, keepdims=True)
    d_ref[...] = d_run

    def pv(p):
        out = _dot_nt(p[:, :PAGE], pages[0][1].astype(BF16))
        for j in range(1, PAGES_PER_STEP):
            out = out + _dot_nt(p[:, j * PAGE:(j + 1) * PAGE], pages[j][1].astype(BF16))
        return out

    _softmax_update(jnp.concatenate(scores, axis=1), pv, m_ref, l_ref, acc_ref)

    @pl.when(p_id == pl.num_programs(1) - 1)
    def _():
        o_all = acc_ref[...] / l_ref[...]
        col_head = _iota((8, width), 1) // D_C
        o = jnp.zeros((8, width), F32)
        for h in range(H_C):
            o = o + jnp.where(col_head == h, o_all[h * 8:(h + 1) * 8, :], 0.0)
        o_ref[...] = o


def _decode_attn(body, page_table, q_bd, new_kv, cache, layer, *, out_rows, out_cols, scratch,
                 reverse, extra=(), extra_specs=(), lf_new=None, lf_cache=None, name):
    nb, n_pages = page_table.shape
    steps = n_pages // PAGES_PER_STEP
    in_specs = list(extra_specs) + [
        pl.BlockSpec((None,) + q_bd.shape[1:], lambda b, p, pt: (b, 0, 0)),
        pl.BlockSpec((None,) + new_kv.shape[1:], lambda b, p, pt: (b, 0, 0))]
    args = list(extra) + [q_bd, new_kv]
    if lf_new is not None:
        in_specs.append(pl.BlockSpec((None,) + lf_new.shape[1:], lambda b, p, pt: (b, 0, 0)))
        args.append(lf_new)
    in_specs += _page_specs(cache.shape[2:], layer, n_pages, reverse)
    args += [cache] * PAGES_PER_STEP
    if lf_cache is not None:
        in_specs += _page_specs((H_C, PAGE), layer, n_pages, reverse)
        args += [lf_cache] * PAGES_PER_STEP
    grid_spec = pltpu.PrefetchScalarGridSpec(
        num_scalar_prefetch=1, grid=(nb, steps), in_specs=in_specs,
        out_specs=pl.BlockSpec((None, out_rows, out_cols), lambda b, p, pt: (b, 0, 0)),
        scratch_shapes=scratch)
    return pl.pallas_call(
        body, grid_spec=grid_spec,
        out_shape=jax.ShapeDtypeStruct((nb, out_rows, out_cols), F32),
        compiler_params=_params("parallel", "arbitrary"), name=name)(page_table, *args)


def _mem_body(q_ref, kv_ref, o_ref, kvb_ref, *, d_m):
    @pl.when(pl.program_id(1) == 0)
    def _():
        kvb_ref[...] = kv_ref[...].astype(BF16)

    hd = H_M * d_m
    for h in range(H_M):
        q = (q_ref[:, h * d_m:(h + 1) * d_m] * (d_m ** -0.5)).astype(BF16)
        s = _dot_nt(q, kvb_ref[:, h * d_m:(h + 1) * d_m])
        p = jnp.exp(s - jnp.max(s, axis=1, keepdims=True))
        o = _dot(p.astype(BF16), kvb_ref[:, hd + h * d_m: hd + (h + 1) * d_m])
        o_ref[:, h * d_m:(h + 1) * d_m] = (o / jnp.sum(p, axis=1, keepdims=True)).astype(o_ref.dtype)


def _mem_attn(q, mem_kv, kv_lead, *, nb, tq, name):
    M, D = q.shape
    nq = M // nb // tq
    n_mem = mem_kv.shape[-2]
    lead = tuple(kv_lead)
    return pl.pallas_call(
        functools.partial(_mem_body, d_m=D // H_M), grid=(nb, nq),
        in_specs=[pl.BlockSpec((tq, D), lambda b, i: (b * nq + i, 0)),
                  pl.BlockSpec((None,) * (len(lead) + 1) + (n_mem, 2 * D), lambda b, i: lead + (b, 0, 0))],
        out_specs=pl.BlockSpec((tq, D), lambda b, i: (b * nq + i, 0)),
        out_shape=jax.ShapeDtypeStruct((M, D), BF16),
        scratch_shapes=[pltpu.VMEM((n_mem, 2 * D), BF16)],
        compiler_params=_params("parallel", "arbitrary"), name=name)(q, mem_kv)


def _mm_ln_body(*refs, n_in, alpha):
    xs = refs[:n_in]
    ws = refs[n_in:2 * n_in]
    r_ref, g_ref, b_ref, o_ref = refs[2 * n_in:]
    y = _dot(xs[0][...].astype(BF16), ws[0][...])
    for x_ref, w_ref in zip(xs[1:], ws[1:]):
        y = y + _dot(x_ref[...].astype(BF16), w_ref[...])
    o_ref[...] = _layer_norm(alpha * r_ref[...] + y, g_ref[...], b_ref[...])


def _mm_ln(xs, w, layer, res, g, b, *, tm, alpha, name):
    M, D = res.shape
    in_specs, args, off = [], [], 0
    for x in xs:
        in_specs.append(pl.BlockSpec((tm, x.shape[1]), lambda m: (m, 0)))
    for x in xs:
        kx = x.shape[1]
        in_specs.append(pl.BlockSpec((None, kx, D), lambda m, o=off // kx: (layer, o, 0)))
        off += kx
    in_specs += [pl.BlockSpec((tm, D), lambda m: (m, 0)),
                 pl.BlockSpec((1, D), lambda m: (0, 0)), pl.BlockSpec((1, D), lambda m: (0, 0))]
    return pl.pallas_call(
        functools.partial(_mm_ln_body, n_in=len(xs), alpha=alpha), grid=(M // tm,),
        in_specs=in_specs, out_specs=pl.BlockSpec((tm, D), lambda m: (m, 0)),
        out_shape=jax.ShapeDtypeStruct((M, D), F32),
        compiler_params=_params("parallel"), name=name)(*xs, *([w] * len(xs)), res, g, b)


def _swiglu_chunk(xb, wg_ref, wu_ref, wd_ref):
    hg = _dot(xb, wg_ref[...])
    hu = _dot(xb, wu_ref[...])
    a = hg * jax.nn.sigmoid(hg) * hu
    return _dot(a.astype(BF16), wd_ref[...])


def _ffn_body(x_ref, wg_ref, wu_ref, wd_ref, g_ref, b_ref, o_ref, xb_ref, acc_ref, *, alpha):
    f = pl.program_id(1)

    @pl.when(f == 0)
    def _():
        xb_ref[...] = x_ref[...].astype(BF16)
        acc_ref[...] = jnp.zeros_like(acc_ref)

    acc_ref[...] += _swiglu_chunk(xb_ref[...], wg_ref, wu_ref, wd_ref)

    @pl.when(f == pl.num_programs(1) - 1)
    def _():
        o_ref[...] = _layer_norm(alpha * x_ref[...] + acc_ref[...], g_ref[...], b_ref[...])


def _ffn(x, w_gu, w_down, layer, g, b, *, tm, tf, alpha, name):
    M, D = x.shape
    nf = w_down.shape[1] // tf
    return pl.pallas_call(
        functools.partial(_ffn_body, alpha=alpha), grid=(M // tm, nf),
        in_specs=[pl.BlockSpec((tm, D), lambda m, f: (m, 0)),
                  pl.BlockSpec((None, D, tf), lambda m, f: (layer, 0, f)),
                  pl.BlockSpec((None, D, tf), lambda m, f: (layer, 0, nf + f)),
                  pl.BlockSpec((None, tf, D), lambda m, f: (layer, f, 0)),
                  pl.BlockSpec((1, D), lambda m, f: (0, 0)), pl.BlockSpec((1, D), lambda m, f: (0, 0))],
        out_specs=pl.BlockSpec((tm, D), lambda m, f: (m, 0)),
        out_shape=jax.ShapeDtypeStruct((M, D), F32),
        scratch_shapes=[pltpu.VMEM((tm, D), BF16), pltpu.VMEM((tm, D), F32)],
        compiler_params=_params("parallel", "arbitrary"), name=name)(x, w_gu, w_gu, w_down, g, b)


def _moe_body(x_ref, wr_ref, wg_ref, wu_ref, wd_ref, g_ref, b_ref, o_ref,
              comb_ref, rank_ref, comb_t_ref, rank_t_ref, xc_ref, yc_ref, acc_ref, cnt_ref, *, alpha):
    e = pl.program_id(1)
    f = pl.program_id(2)
    last_f = f == pl.num_programs(2) - 1
    T = x_ref.shape[0]

    @pl.when((e == 0) & (f == 0))
    def _():
        x = x_ref[...]
        acc_ref[...] = jnp.zeros_like(acc_ref)
        logits = jnp.dot(x, wr_ref[...], precision=lax.Precision.HIGHEST, preferred_element_type=F32)
        lane = _iota(logits.shape, 1).astype(F32)
        logits = jnp.where(lane < N_EXPERTS, logits, NEG)
        m1 = jnp.max(logits, axis=1, keepdims=True)
        i1 = jnp.min(jnp.where(logits == m1, lane, float(LANES)), axis=1, keepdims=True)
        rest = jnp.where(lane == i1, NEG, logits)
        m2 = jnp.max(rest, axis=1, keepdims=True)
        i2 = jnp.min(jnp.where(rest == m2, lane, float(LANES)), axis=1, keepdims=True)
        e2 = jnp.exp(m2 - m1)
        den = 1.0 + e2
        comb = jnp.where(lane == i1, 1.0 / den, 0.0) + jnp.where(lane == i2, e2 / den, 0.0)
        comb_ref[...] = comb
        comb_t = comb.T
        comb_t_ref[...] = comb_t
        r = _iota((T, T), 0)
        c = _iota((T, T), 1)
        rank_ref[...] = _dot((c < r).astype(BF16), (comb > 0.0).astype(BF16))
        rank_t_ref[...] = _dot((comb_t > 0.0).astype(BF16), (r < c).astype(BF16))

    sub = _iota((LANES, T), 0)

    @pl.when(f == 0)
    def _():
        comb_t = comb_t_ref[...]
        routed_t = jnp.sum(jnp.where((sub == e) & (comb_t > 0.0), 1.0, 0.0), axis=0, keepdims=True)
        rank_t = jnp.sum(jnp.where(sub == e, rank_t_ref[...], 0.0), axis=0, keepdims=True)
        n_rows = jnp.sum(routed_t).astype(jnp.int32)
        cnt_ref[0] = (n_rows + MOE_CHUNK - 1) // MOE_CHUNK
        xb = x_ref[...].astype(BF16)
        slot = _iota((MOE_CHUNK, T), 0).astype(F32)

        def compact(ci, carry):
            base = pl.multiple_of(ci * MOE_CHUNK, MOE_CHUNK)
            pick = ((rank_t == slot + base.astype(F32)) & (routed_t > 0.0)).astype(BF16)
            xc_ref[pl.ds(base, MOE_CHUNK), :] = _dot(pick, xb).astype(BF16)
            yc_ref[pl.ds(base, MOE_CHUNK), :] = jnp.zeros((MOE_CHUNK, yc_ref.shape[1]), F32)
            return carry

        lax.fori_loop(0, cnt_ref[0], compact, 0)

    def expert(ci, carry):
        base = pl.multiple_of(ci * MOE_CHUNK, MOE_CHUNK)
        yc_ref[pl.ds(base, MOE_CHUNK), :] += _swiglu_chunk(xc_ref[pl.ds(base, MOE_CHUNK), :], wg_ref, wu_ref, wd_ref)
        return carry

    lax.fori_loop(0, cnt_ref[0], expert, 0)

    @pl.when(last_f)
    def _():
        comb = comb_ref[...]
        lane = _iota(comb.shape, 1)
        gate = jnp.sum(jnp.where(lane == e, comb, 0.0), axis=1, keepdims=True)
        rank = jnp.sum(jnp.where(lane == e, rank_ref[...], 0.0), axis=1, keepdims=True)
        slot = _iota((T, MOE_CHUNK), 1).astype(F32)

        def scatter(ci, carry):
            base = pl.multiple_of(ci * MOE_CHUNK, MOE_CHUNK)
            place = ((rank == slot + base.astype(F32)) & (gate > 0.0)).astype(BF16)
            acc_ref[...] += gate * _dot(place, yc_ref[pl.ds(base, MOE_CHUNK), :].astype(BF16))
            return carry

        lax.fori_loop(0, cnt_ref[0], scatter, 0)

    @pl.when(last_f & (e == pl.num_programs(1) - 1))
    def _():
        o_ref[...] = _layer_norm(alpha * x_ref[...] + acc_ref[...], g_ref[...], b_ref[...])


def _moe(x, w_router, w_gu, w_down, layer, g, b, *, tm, tf, alpha, name):
    M, D = x.shape
    nf = w_down.shape[2] // tf
    assert tm % MOE_CHUNK == 0 and M % tm == 0
    once = pl.Buffered(1)
    return pl.pallas_call(
        functools.partial(_moe_body, alpha=alpha), grid=(M // tm, N_EXPERTS, nf),
        in_specs=[pl.BlockSpec((tm, D), lambda m, e, f: (m, 0), pipeline_mode=once),
                  pl.BlockSpec((None, D, LANES), lambda m, e, f: (layer, 0, 0), pipeline_mode=once),
                  pl.BlockSpec((None, None, D, tf), lambda m, e, f: (layer, e, 0, f)),
                  pl.BlockSpec((None, None, D, tf), lambda m, e, f: (layer, e, 0, nf + f)),
                  pl.BlockSpec((None, None, tf, D), lambda m, e, f: (layer, e, f, 0)),
                  pl.BlockSpec((1, D), lambda m, e, f: (0, 0)), pl.BlockSpec((1, D), lambda m, e, f: (0, 0))],
        out_specs=pl.BlockSpec((tm, D), lambda m, e, f: (m, 0)),
        out_shape=jax.ShapeDtypeStruct((M, D), F32),
        scratch_shapes=[pltpu.VMEM((tm, LANES), F32), pltpu.VMEM((tm, LANES), F32),
                        pltpu.VMEM((LANES, tm), F32), pltpu.VMEM((LANES, tm), F32),
                        pltpu.VMEM((tm, D), BF16), pltpu.VMEM((tm, D), F32), pltpu.VMEM((tm, D), F32),
                        pltpu.SMEM((1,), jnp.int32)],
        compiler_params=_params("parallel", "arbitrary", "arbitrary"), name=name)(
            x, w_router, w_gu, w_gu, w_down, g, b)


def _rope_tables(pos):
    half = DK_A // 2
    inv = 1.0 / (ROPE_THETA ** (jnp.arange(half, dtype=F32) / half))
    ang = pos.astype(F32)[:, None] * inv[None, :]
    cos, sin = jnp.cos(ang), jnp.sin(ang)
    return jnp.concatenate([cos] * 4, axis=1), jnp.concatenate([-sin, sin, -sin, sin], axis=1)


def _block_diag_queries(q, nb, n_heads, n_maps, d, rows_per_head, scale):
    g = n_heads * n_maps
    lq = q.shape[0] // nb
    q5 = (q * scale).reshape(nb, lq, g, d)
    q5 = jnp.pad(q5, ((0, 0), (0, rows_per_head - lq), (0, 0), (0, 0)))
    eye = jnp.eye(g, dtype=q.dtype)
    out = jnp.einsum('bigd,gx->bgixd', q5, eye)
    return out.reshape(nb, g * rows_per_head, g * d).astype(BF16)


def _pad_rows(x, nb, rows):
    lq = x.shape[0] // nb
    return jnp.pad(x.reshape(nb, lq, x.shape[1]), ((0, 0), (0, rows - lq), (0, 0)))


def kernel(x_prompt, x_sample, cache_a_kv, cache_b_kv, cache_c_kv, cache_c_logf, cache_mem_kv,
           page_table, mem_prompt, even_w_in, even_w_out, diff_lambda, diff_subln_g,
           odd_w_in, odd_b_f, odd_w_out, mem_w_q, mem_w_kv, mem_w_o, ffn_w_gu, ffn_w_down,
           moe_w_router, moe_w_gu, moe_w_down, ln_g, ln_b):
    depth = ln_g.shape[0]
    alpha = (2 * depth) ** 0.25
    nbp, L, D = x_prompt.shape
    nbs, Ls, _ = x_sample.shape
    n_pages = page_table.shape[1]
    n_pool = cache_a_kv.shape[1]
    past_len = n_pages * PAGE
    n_mem = mem_prompt.shape[1]
    assert Ls == 8 and n_pages % PAGES_PER_STEP == 0
    qa_w = H_A * 2 * DK_A
    qb_w = H_B * D_B
    qc_w = H_C * D_C
    even_in = 3 * qa_w + 3 * qb_w
    odd_main = 3 * qc_w

    w_even_in = even_w_in.astype(BF16)
    w_even_out = even_w_out.astype(BF16)
    w_odd_in = odd_w_in[:, :, :odd_main].astype(BF16)
    w_odd_f = jnp.pad(odd_w_in[:, :, odd_main:], ((0, 0), (0, 0), (0, LANES - H_C))).astype(BF16)
    b_odd_f = jnp.pad(odd_b_f, ((0, 0), (0, LANES - H_C)))[:, None, :]
    w_odd_out = odd_w_out.astype(BF16)
    w_mem_q = mem_w_q.astype(BF16)
    w_mem_kv = mem_w_kv.astype(BF16)
    w_mem_o = mem_w_o.astype(BF16)
    w_ffn_gu = ffn_w_gu.astype(BF16)
    w_ffn_down = ffn_w_down.astype(BF16)
    w_moe_gu = moe_w_gu.astype(BF16)
    w_moe_down = moe_w_down.astype(BF16)
    w_router = jnp.pad(moe_w_router, ((0, 0), (0, 0), (0, LANES - N_EXPERTS)))

    ca = cache_a_kv.reshape(cache_a_kv.shape[0], n_pool, PAGE * 2 * H_A, 2 * DK_A)
    cb = jnp.transpose(cache_b_kv, (0, 1, 3, 4, 5, 2)).reshape(cache_b_kv.shape[0], n_pool, 2, qb_w, PAGE)
    cc = jnp.transpose(cache_c_kv, (0, 1, 3, 4, 5, 2)).reshape(cache_c_kv.shape[0], n_pool, 2, qc_w, PAGE)
    clf = jnp.swapaxes(cache_c_logf, 2, 3)

    rope_p = _rope_tables(jnp.arange(L))
    rope_s = _rope_tables(past_len + (jnp.arange(nbs * Ls) % Ls))

    xp = x_prompt.reshape(nbp * L, D)
    xs = x_sample.reshape(nbs * Ls, D)
    mp = mem_prompt.reshape(nbp * n_mem, D)
    Mp, Ms = xp.shape[0], xs.shape[0]
    tmp, tms = min(1024, Mp), Ms
    tq = min(512, L)
    tk = 2 * tq if (L // tq) % 2 == 0 else tq
    tq_sb = min(256, L)

    a_p, a_s, b_p, b_s, c_p, c_s, lf_p, lf_s, mem_rows = [], [], [], [], [], [], [], [], []
    for l in range(depth):
        i = l // 2
        g3 = ln_g[l][:, None, :]
        b3 = ln_b[l][:, None, :]
        mem_kv_p = _proj(mp, w_mem_kv, l, 2 * D, tm=min(1024, mp.shape[0]), name=f"memkv{l}")
        mem_rows.append(mem_kv_p.reshape(nbp, n_mem, 2, H_M, D // H_M))
        mem_kv_p = mem_kv_p.reshape(nbp, n_mem, 2 * D)
        if l % 2 == 0:
            lam_init = _lambda_init(l)
            lv = diff_lambda[i]
            sg = diff_subln_g[i][None, :]
            hp = _proj(xp, w_even_in, i, even_in, tm=tmp, rope=rope_p, rope_cols=2 * qa_w, name=f"even_in_p{l}")
            a_p.append(hp[:, qa_w:3 * qa_w].reshape(nbp, L, 2, H_A, 2 * DK_A))
            b_p.append(hp[:, 3 * qa_w + qb_w:].reshape(nbp, L, 2, H_B, D_B))
            oa = _prompt_attn(functools.partial(_diff_body, tq=tq, tk=tk, lam_init=lam_init), hp, nb=nbp, L=L,
                              n_groups=H_A, qcol=0, kcol=H_A, vcol=2 * H_A, tq=tq, scratch=(), extra=(lv, sg),
                              extra_specs=(pl.BlockSpec(lv.shape, lambda b, g, i: (0, 0)),
                                           pl.BlockSpec(sg.shape, lambda b, g, i: (0, 0))),
                              name=f"diff_p{l}")
            ob = _prompt_attn(functools.partial(_sb_body, tq=tq_sb), hp, nb=nbp, L=L, n_groups=H_B // 2,
                              qcol=3 * H_A, kcol=3 * H_A + H_B // 2, vcol=3 * H_A + H_B, tq=tq_sb, scratch=(),
                              name=f"stick_p{l}")
            xp = _mm_ln([oa, ob], w_even_out, i, xp, g3[0], b3[0], tm=tmp, alpha=alpha, name=f"even_out_p{l}")
            hs = _proj(xs, w_even_in, i, even_in, tm=tms, rope=rope_s, rope_cols=2 * qa_w, name=f"even_in_s{l}")
            a_new = hs[:, qa_w:3 * qa_w]
            b_new = hs[:, 3 * qa_w + qb_w:]
            a_s.append(a_new.reshape(nbs, Ls, 2, H_A, 2 * DK_A))
            b_s.append(b_new.reshape(nbs, Ls, 2, H_B, D_B))
            qa_bd = _block_diag_queries(hs[:, :qa_w], nbs, H_A, 2, DK_A, 16, DK_A ** -0.5)
            qb_bd = _block_diag_queries(hs[:, 3 * qa_w:3 * qa_w + qb_w], nbs, H_B, 1, D_B, 16, D_B ** -0.5)
            oa = _decode_attn(functools.partial(_dec_diff_body, lam_init=lam_init), page_table, qa_bd,
                              _pad_rows(a_new, nbs, PAGE), ca, i, out_rows=16, out_cols=qa_w,
                              scratch=[pltpu.VMEM((LANES, 1), F32), pltpu.VMEM((LANES, 1), F32),
                                       pltpu.VMEM((LANES, qa_w), F32)],
                              reverse=False, extra=(lv, sg),
                              extra_specs=(pl.BlockSpec(lv.shape, lambda b, p, pt: (0, 0)),
                                           pl.BlockSpec(sg.shape, lambda b, p, pt: (0, 0))),
                              name=f"diff_s{l}")
            ob = _decode_stick(page_table, qb_bd, _pad_rows(b_new, nbs, PAGE), cb, i, name=f"stick_s{l}")
            oa = oa[:, :Ls].reshape(Ms, qa_w)
            ob = ob[:, :Ls].reshape(Ms, qb_w)
            xs = _mm_ln([oa, ob], w_even_out, i, xs, g3[0], b3[0], tm=tms, alpha=alpha, name=f"even_out_s{l}")
        else:
            hp = _proj(xp, w_odd_in, i, odd_main, tm=tmp, name=f"odd_in_p{l}")
            c_p.append(hp[:, qc_w:].reshape(nbp, L, 2, H_C, D_C))
            lf, cs = _logf(xp, w_odd_f[i], b_odd_f[i], nb=nbp, tl=tq_sb, with_cumsum=True, name=f"logf_p{l}")
            lf_p.append(lf[:, :H_C].reshape(nbp, L, H_C))
            nq = L // tq
            oc = _prompt_attn(functools.partial(_fox_body, tq=tq, tk=tk), hp, nb=nbp, L=L, n_groups=H_C // 2,
                              qcol=0, kcol=H_C // 2, vcol=H_C, tq=tq, scratch=(2,), extra=(cs, cs),
                              extra_specs=(pl.BlockSpec((tq, LANES), lambda b, g, i: (b * nq + i, 0)),
                                           pl.BlockSpec((L, LANES), lambda b, g, i: (b, 0))),
                              name=f"fox_p{l}")
            xp = _mm_ln([oc], w_odd_out, i, xp, g3[0], b3[0], tm=tmp, alpha=alpha, name=f"odd_out_p{l}")
            hs = _proj(xs, w_odd_in, i, odd_main, tm=tms, name=f"odd_in_s{l}")
            c_new = hs[:, qc_w:]
            c_s.append(c_new.reshape(nbs, Ls, 2, H_C, D_C))
            lf, _ = _logf(xs, w_odd_f[i], b_odd_f[i], nb=1, tl=Ms, with_cumsum=False, name=f"logf_s{l}")
            lf_s.append(lf[:, :H_C].reshape(nbs, Ls, H_C))
            lf_new_t = jnp.swapaxes(_pad_rows(lf[:, :H_C], nbs, PAGE), 1, 2)
            qc_bd = _block_diag_queries(hs[:, :qc_w], nbs, H_C, 1, D_C, 8, D_C ** -0.5)
            oc = _decode_attn(_dec_fox_body, page_table, qc_bd, _pad_rows(c_new, nbs, PAGE), cc, i,
                              out_rows=8, out_cols=qc_w,
                              scratch=[pltpu.VMEM((LANES, 1), F32), pltpu.VMEM((LANES, 1), F32),
                                       pltpu.VMEM((LANES, qc_w), F32), pltpu.VMEM((H_C, 1), F32)],
                              reverse=True, lf_new=lf_new_t, lf_cache=clf, name=f"fox_s{l}")
            oc = oc.reshape(Ms, qc_w)
            xs = _mm_ln([oc], w_odd_out, i, xs, g3[0], b3[0], tm=tms, alpha=alpha, name=f"odd_out_s{l}")

        qm = _proj(xp, w_mem_q, l, D, tm=tmp, name=f"mem_q_p{l}")
        om = _mem_attn(qm, mem_kv_p, (), nb=nbp, tq=tq, name=f"mem_attn_p{l}")
        xp = _mm_ln([om], w_mem_o, l, xp, g3[1], b3[1], tm=tmp, alpha=alpha, name=f"mem_out_p{l}")
        qm = _proj(xs, w_mem_q, l, D, tm=tms, name=f"mem_q_s{l}")
        om = _mem_attn(qm, cache_mem_kv.reshape(depth, nbs, n_mem, 2 * D), (l,), nb=nbs, tq=Ls,
                       name=f"mem_attn_s{l}")
        xs = _mm_ln([om], w_mem_o, l, xs, g3[1], b3[1], tm=tms, alpha=alpha, name=f"mem_out_s{l}")

        if l % 2 == 0:
            xp = _ffn(xp, w_ffn_gu, w_ffn_down, i, g3[2], b3[2], tm=min(1024, Mp), tf=256, alpha=alpha, name=f"ffn_p{l}")
            xs = _ffn(xs, w_ffn_gu, w_ffn_down, i, g3[2], b3[2], tm=tms, tf=256, alpha=alpha, name=f"ffn_s{l}")
        else:
            tf_moe = w_moe_down.shape[2] // 2
            xp = _moe(xp, w_router, w_moe_gu, w_moe_down, i, g3[2], b3[2], tm=min(1024, Mp), tf=tf_moe,
                      alpha=alpha, name=f"moe_p{l}")
            xs = _moe(xs, w_router, w_moe_gu, w_moe_down, i, g3[2], b3[2], tm=tms, tf=tf_moe, alpha=alpha,
                      name=f"moe_s{l}")

    return (xp.reshape(nbp, L, D), xs.reshape(nbs, Ls, D),
            jnp.stack(a_p), jnp.stack(a_s), jnp.stack(b_p), jnp.stack(b_s),
            jnp.stack(c_p), jnp.stack(c_s), jnp.stack(lf_p), jnp.stack(lf_s), jnp.stack(mem_rows))
```

```python
import functools
import math

import jax
import jax.numpy as jnp
from jax import lax
from jax.experimental import pallas as pl
from jax.experimental.pallas import tpu as pltpu

F32 = jnp.float32
BF16 = jnp.bfloat16

H_A, DK_A = 4, 64
H_B, D_B = 8, 64
H_C, D_C = 16, 64
H_M = 4
N_EXPERTS, TOP_K = 8, 2
ROPE_THETA = 10000.0
LN_EPS = 1e-5
RMS_EPS = 1e-5
PAGE = 128
LANES = 128
NEG = -1e30
SB_CUTOFF = -104.0
VMEM_LIMIT = 56 << 20
PAGES_PER_STEP = 8
MOE_CHUNK = 128


def _lambda_init(layer_idx):
    return 0.8 - 0.6 * math.exp(-0.3 * layer_idx)


def _params(*sem):
    return pltpu.CompilerParams(dimension_semantics=sem, vmem_limit_bytes=VMEM_LIMIT)


def _dot(a, b):
    return jnp.dot(a, b, preferred_element_type=F32)


def _dot_nt(a, b):
    return lax.dot_general(a, b, (((1,), (1,)), ((), ())), preferred_element_type=F32)


def _log_sigmoid(z):
    return jnp.minimum(z, 0.0) - jnp.log1p(jnp.exp(-jnp.abs(z)))


def _layer_norm(v, g, b):
    mu = jnp.mean(v, axis=1, keepdims=True)
    d = v - mu
    var = jnp.mean(d * d, axis=1, keepdims=True)
    return d * lax.rsqrt(var + LN_EPS) * g + b


def _split2(x):
    hi = x.astype(BF16)
    lo = (x - hi.astype(F32)).astype(BF16)
    return hi, lo


def _split3(x):
    h1 = x.astype(BF16)
    r1 = x - h1.astype(F32)
    h2 = r1.astype(BF16)
    h3 = (r1 - h2.astype(F32)).astype(BF16)
    return h1, h2, h3


def _iota(shape, dim):
    return lax.broadcasted_iota(jnp.int32, shape, dim)


def _proj_body(*refs, n_rope):
    if n_rope:
        x_ref, w_ref, cos_ref, sin_ref, o_ref = refs
    else:
        x_ref, w_ref, o_ref = refs
    h = _dot(x_ref[...].astype(BF16), w_ref[...])
    if not n_rope:
        o_ref[...] = h
        return
    n = pl.program_id(1)
    tn = h.shape[1]

    @pl.when(n < n_rope)
    def _():
        reps = tn // LANES
        c = jnp.concatenate([cos_ref[...]] * reps, axis=1)
        s = jnp.concatenate([sin_ref[...]] * reps, axis=1)
        lane = _iota(h.shape, 1)
        first = (lane % 64) < 32
        rot = jnp.where(first, pltpu.roll(h, tn - 32, 1), pltpu.roll(h, 32, 1))
        o_ref[...] = h * c + rot * s

    @pl.when(n >= n_rope)
    def _():
        o_ref[...] = h


def _proj(x, w, layer, n_cols, *, tm, rope=None, rope_cols=0, name):
    M, K = x.shape
    tn = 1024 if n_cols % 1024 == 0 and rope_cols % 1024 == 0 else 512
    n_rope = rope_cols // tn
    grid = (M // tm, n_cols // tn)
    in_specs = [pl.BlockSpec((tm, K), lambda m, n: (m, 0)),
                pl.BlockSpec((None, K, tn), lambda m, n: (layer, 0, n))]
    args = [x, w]
    if n_rope:
        cos, sin = rope
        nt = cos.shape[0] // tm
        in_specs += [pl.BlockSpec((tm, LANES), lambda m, n: (m % nt, 0)),
                     pl.BlockSpec((tm, LANES), lambda m, n: (m % nt, 0))]
        args += [cos, sin]
    return pl.pallas_call(
        functools.partial(_proj_body, n_rope=n_rope),
        grid=grid, in_specs=in_specs,
        out_specs=pl.BlockSpec((tm, tn), lambda m, n: (m, n)),
        out_shape=jax.ShapeDtypeStruct((M, n_cols), F32),
        compiler_params=_params("parallel", "arbitrary"), name=name)(*args)


def _logf_body(x_ref, w_ref, b_ref, lf_ref, cs_ref, carry_ref, *, with_cumsum):
    t = pl.program_id(1)

    @pl.when(t == 0)
    def _():
        carry_ref[...] = jnp.zeros_like(carry_ref)

    f = _dot(x_ref[...].astype(BF16), w_ref[...]) + b_ref[...]
    lf = _log_sigmoid(f)
    lf_ref[...] = lf
    tl = lf.shape[0]
    if not with_cumsum:
        cs_ref[...] = lf
        return
    tri = (_iota((tl, tl), 1) <= _iota((tl, tl), 0)).astype(BF16)
    parts = _split3(lf)
    cs = carry_ref[...] + _dot(tri, parts[0]) + _dot(tri, parts[1]) + _dot(tri, parts[2])
    cs_ref[...] = cs
    carry_ref[...] = cs[tl - 1:tl, :]


def _logf(x, wf, bf, *, nb, tl, with_cumsum, name):
    M, K = x.shape
    nt = M // nb // tl
    return pl.pallas_call(
        functools.partial(_logf_body, with_cumsum=with_cumsum), grid=(nb, nt),
        in_specs=[pl.BlockSpec((tl, K), lambda b, t: (b * nt + t, 0)),
                  pl.BlockSpec((K, LANES), lambda b, t: (0, 0)),
                  pl.BlockSpec((1, LANES), lambda b, t: (0, 0))],
        out_specs=[pl.BlockSpec((tl, LANES), lambda b, t: (b * nt + t, 0)),
                   pl.BlockSpec((tl, LANES), lambda b, t: (b * nt + t, 0))],
        out_shape=[jax.ShapeDtypeStruct((M, LANES), F32)] * 2,
        scratch_shapes=[pltpu.VMEM((1, LANES), F32)],
        compiler_params=_params("parallel", "arbitrary"), name=name)(x, wf, bf)


def _load_kv(k_ref, v_ref, kb_ref, vb_ref):
    @pl.when(pl.program_id(2) == 0)
    def _():
        kb_ref[...] = k_ref[...].astype(BF16)
        vb_ref[...] = v_ref[...].astype(BF16)


def _softmax_step(s, v, m, l, acc):
    m_new = jnp.maximum(m, jnp.max(s, axis=1, keepdims=True))
    alpha = jnp.exp(m - m_new)
    p = jnp.exp(s - m_new)
    l = alpha * l + jnp.sum(p, axis=1, keepdims=True)
    acc = alpha * acc + _dot(p.astype(BF16), v)
    return m_new, l, acc


def _causal_mask(q0, k0, tq, width):
    return k0 + _iota((tq, width), 1) <= q0 + _iota((tq, width), 0)


def _causal_sweep(step, qi, tq, tk, init):
    n_full = (qi * tq) // tk
    carry = lax.fori_loop(0, n_full, lambda i, c: step(pl.multiple_of(i * tk, tk), tk, c, False), init)
    off = pl.multiple_of(n_full * tk, tk)
    if tk == tq:
        return step(off, tq, carry, True)
    return lax.cond((qi * tq) % tk == 0,
                    lambda c: step(off, tq, c, True), lambda c: step(off, tk, c, True), carry)


def _diff_body(lv_ref, g_ref, q_ref, k_ref, v_ref, o_ref, kb_ref, vb_ref, *, tq, tk, lam_init):
    _load_kv(k_ref, v_ref, kb_ref, vb_ref)
    qi = pl.program_id(2)
    q = q_ref[...] * (DK_A ** -0.5)
    lane = _iota(q.shape, 1)
    qs = (jnp.where(lane < 64, q, 0.0).astype(BF16), jnp.where(lane >= 64, q, 0.0).astype(BF16))

    def step(off, width, carry, masked):
        k = kb_ref[pl.ds(off, width), :]
        v = vb_ref[pl.ds(off, width), :]
        out = []
        for c in range(2):
            s = _dot_nt(qs[c], k)
            if masked:
                s = jnp.where(_causal_mask(qi * tq, off, tq, width), s, NEG)
            out.append(_softmax_step(s, v, *carry[c]))
        return tuple(out)

    z1 = jnp.zeros((tq, 1), F32)
    init = ((z1 + NEG, z1, jnp.zeros((tq, LANES), F32)),) * 2
    (m0, l0, a0), (m1, l1, a1) = _causal_sweep(step, qi, tq, tk, init)
    lv = lv_ref[...]
    lam = (jnp.exp(jnp.sum(lv[0:1, :] * lv[1:2, :], axis=1, keepdims=True))
           - jnp.exp(jnp.sum(lv[2:3, :] * lv[3:4, :], axis=1, keepdims=True)) + lam_init)
    o = a0 / l0 - lam * (a1 / l1)
    o = o * lax.rsqrt(jnp.mean(o * o, axis=1, keepdims=True) + RMS_EPS) * g_ref[...]
    o_ref[...] = (o * (1.0 - lam_init)).astype(o_ref.dtype)


def _sb_body(q_ref, k_ref, v_ref, o_ref, kb_ref, vb_ref, *, tq):
    _load_kv(k_ref, v_ref, kb_ref, vb_ref)
    qi = pl.program_id(2)
    q = q_ref[...] * (D_B ** -0.5)
    lane = _iota(q.shape, 1)
    rows = _iota((tq, tq), 0)
    cols = _iota((tq, tq), 1)
    later = (rows > cols).astype(BF16)
    strict = cols < rows
    qs = [jnp.where((lane >= 64 * hh) & (lane < 64 * hh + 64), q, 0.0).astype(BF16) for hh in range(2)]

    def step(ki, carry, diag):
        off = pl.multiple_of(ki * tq, tq)
        k = kb_ref[pl.ds(off, tq), :]
        v = vb_ref[pl.ds(off, tq), :]
        out = []
        for hh in range(2):
            c_run, acc = carry[hh]
            z = _dot_nt(qs[hh], k)
            ls = _log_sigmoid(z)
            lom = ls - z
            if diag:
                lom = jnp.where(strict, lom, 0.0)
            hi, lo = _split2(lom)
            suffix = _dot(hi, later) + _dot(lo, later)
            a = jnp.exp(ls + suffix + c_run)
            if diag:
                a = jnp.where(strict, a, 0.0)
            out.append((c_run + jnp.sum(lom, axis=1, keepdims=True), acc + _dot(a.astype(BF16), v)))
        return tuple(out)

    def live(carry):
        return jnp.max(jnp.maximum(carry[0][0], carry[1][0])) > SB_CUTOFF

    init = ((jnp.zeros((tq, 1), F32), jnp.zeros((tq, LANES), F32)),) * 2
    carry = step(qi, init, True)
    _, carry = lax.while_loop(lambda st: (st[0] < qi) & live(st[1]),
                              lambda st: (st[0] + 1, step(qi - 1 - st[0], st[1], False)),
                              (jnp.int32(0), carry))
    o_ref[...] = jnp.where(lane < 64, carry[0][1], carry[1][1]).astype(o_ref.dtype)


def _fox_extra_lanes(lane, hh, parts, ones_first):
    e0 = 64 * (1 - hh)
    one = jnp.where((lane >= e0 + (0 if ones_first else 3)) & (lane < e0 + (3 if ones_first else 6)), 1.0, 0.0)
    p0 = e0 + (3 if ones_first else 0)
    return (one + jnp.where(lane == p0, parts[0], 0.0) + jnp.where(lane == p0 + 1, parts[1], 0.0)
            + jnp.where(lane == p0 + 2, parts[2], 0.0))


def _split3_f32(x):
    h1 = x.astype(BF16).astype(F32)
    r1 = x - h1
    h2 = r1.astype(BF16).astype(F32)
    return h1, h2, (r1 - h2).astype(BF16).astype(F32)


def _fox_body(fq_ref, fk_ref, q_ref, k_ref, v_ref, o_ref, ka_ref, va_ref, *, tq, tk):
    g = pl.program_id(1)
    qi = pl.program_id(2)

    @pl.when(qi == 0)
    def _():
        k = k_ref[...]
        v = v_ref[...]
        f_all = fk_ref[...]
        lane = _iota(k.shape, 1)
        for hh in range(2):
            own = (lane >= 64 * hh) & (lane < 64 * hh + 64)
            f = jnp.sum(jnp.where(lane == 2 * g + hh, f_all, 0.0), axis=1, keepdims=True)
            parts = [-p for p in _split3_f32(f)]
            ka_ref[hh] = jnp.where(own, k, _fox_extra_lanes(lane, hh, parts, True)).astype(BF16)
            va_ref[hh] = jnp.where(own, v, jnp.where(lane == 64 * (1 - hh), 1.0, 0.0)).astype(BF16)

    q = q_ref[...] * (D_C ** -0.5)
    lane = _iota(q.shape, 1)
    fq_all = fq_ref[...]
    qs = []
    for hh in range(2):
        own = (lane >= 64 * hh) & (lane < 64 * hh + 64)
        fq = jnp.sum(jnp.where(lane == 2 * g + hh, fq_all, 0.0), axis=1, keepdims=True)
        qs.append(jnp.where(own, q, _fox_extra_lanes(lane, hh, _split3_f32(fq), False)).astype(BF16))

    def step(off, width, carry, masked):
        out = []
        for hh in range(2):
            m, acc = carry[hh]
            s = _dot_nt(qs[hh], ka_ref[hh, pl.ds(off, width), :])
            if masked:
                s = jnp.where(_causal_mask(qi * tq, off, tq, width), s, NEG)
            m_new = jnp.maximum(m, jnp.max(s, axis=1, keepdims=True))
            p = jnp.exp(s - m_new).astype(BF16)
            out.append((m_new, jnp.exp(m - m_new) * acc + _dot(p, va_ref[hh, pl.ds(off, width), :])))
        return tuple(out)

    init = ((jnp.full((tq, 1), NEG, F32), jnp.zeros((tq, LANES), F32)),) * 2
    (_, a0), (_, a1) = _causal_sweep(step, qi, tq, tk, init)
    o_ref[...] = jnp.where(lane < 64, a0 / a0[:, 64:65], a1 / a1[:, 0:1]).astype(o_ref.dtype)


def _prompt_attn(body, h_all, *, nb, L, n_groups, qcol, kcol, vcol, tq, scratch, extra=(), extra_specs=(), name):
    nq = L // tq
    in_specs = list(extra_specs) + [
        pl.BlockSpec((tq, LANES), lambda b, g, i: (b * nq + i, qcol + g)),
        pl.BlockSpec((L, LANES), lambda b, g, i: (b, kcol + g)),
        pl.BlockSpec((L, LANES), lambda b, g, i: (b, vcol + g))]
    return pl.pallas_call(
        body, grid=(nb, n_groups, nq), in_specs=in_specs,
        out_specs=pl.BlockSpec((tq, LANES), lambda b, g, i: (b * nq + i, g)),
        out_shape=jax.ShapeDtypeStruct((nb * L, n_groups * LANES), BF16),
        scratch_shapes=[pltpu.VMEM(scratch + (L, LANES), BF16), pltpu.VMEM(scratch + (L, LANES), BF16)],
        compiler_params=_params("parallel", "parallel", "arbitrary"), name=name)(*extra, h_all, h_all, h_all)


def _page_specs(block, layer, n_pages, reverse):
    specs = []
    for j in range(PAGES_PER_STEP):
        def idx(b, p, pt, j=j):
            lp = p * PAGES_PER_STEP + j
            if reverse:
                lp = n_pages - 1 - lp
            return (layer, pt[b, lp]) + (0,) * len(block)
        specs.append(pl.BlockSpec((None, None) + block, idx))
    return specs


def _softmax_update(s, pv, m_ref, l_ref, acc_ref):
    m = m_ref[...]
    m_new = jnp.maximum(m, jnp.max(s, axis=1, keepdims=True))
    alpha = jnp.exp(m - m_new)
    p = jnp.exp(s - m_new)
    m_ref[...] = m_new
    l_ref[...] = alpha * l_ref[...] + jnp.sum(p, axis=1, keepdims=True)
    acc_ref[...] = alpha * acc_ref[...] + pv(p.astype(BF16))


def _dec_diff_body(pt_ref, lv_ref, g_ref, q_ref, new_ref, *rest, lam_init):
    pages = rest[:PAGES_PER_STEP]
    o_ref, m_ref, l_ref, acc_ref = rest[PAGES_PER_STEP:]
    p_id = pl.program_id(1)
    q = q_ref[...]
    width = H_A * 2 * DK_A

    def update(ks, vs, mask):
        s = jnp.concatenate([_dot_nt(q, k.astype(BF16)) for k in ks], axis=1)
        if mask is not None:
            s = jnp.where(mask, s, NEG)

        def pv(p):
            out = _dot(p[:, :PAGE], vs[0].astype(BF16))
            for j in range(1, len(vs)):
                out = out + _dot(p[:, j * PAGE:(j + 1) * PAGE], vs[j].astype(BF16))
            return out

        _softmax_update(s, pv, m_ref, l_ref, acc_ref)

    @pl.when(p_id == 0)
    def _():
        m_ref[...] = jnp.full_like(m_ref, NEG)
        l_ref[...] = jnp.zeros_like(l_ref)
        acc_ref[...] = jnp.zeros_like(acc_ref)
        i = _iota((LANES, PAGE), 0) % 16
        j = _iota((LANES, PAGE), 1)
        kv = new_ref[...]
        update([kv[:, :width]], [kv[:, width:]], (j <= i) & (j < 8))

    def head_rows(pg, kv):
        return jnp.concatenate([pg[pl.ds(kv * H_A + h, PAGE, stride=2 * H_A), :] for h in range(H_A)], axis=1)

    update([head_rows(pg, 0) for pg in pages], [head_rows(pg, 1) for pg in pages], None)

    @pl.when(p_id == pl.num_programs(1) - 1)
    def _():
        lv = lv_ref[...]
        lam = (jnp.exp(jnp.sum(lv[0:1, :] * lv[1:2, :], axis=1, keepdims=True))
               - jnp.exp(jnp.sum(lv[2:3, :] * lv[3:4, :], axis=1, keepdims=True)) + lam_init)
        o_all = acc_ref[...] / l_ref[...]
        for h in range(H_A):
            r0 = (2 * h) * 16
            cs = slice(h * LANES, (h + 1) * LANES)
            o = o_all[r0:r0 + 16, cs] - lam * o_all[r0 + 16:r0 + 32, cs]
            o = o * lax.rsqrt(jnp.mean(o * o, axis=1, keepdims=True) + RMS_EPS) * g_ref[...]
            o_ref[:, cs] = o * (1.0 - lam_init)


def _dec_sb_body(pt_ref, lim_ref, q_ref, new_ref, c_in_ref, acc_in_ref, *rest, first):
    pages = rest[:PAGES_PER_STEP]
    o_ref, c_out_ref, acc_out_ref, c_ref, acc_ref = rest[PAGES_PER_STEP:]
    p_id = pl.program_id(1)
    q = q_ref[...]
    width = H_B * D_B
    later = (_iota((PAGE, PAGE), 0) > _iota((PAGE, PAGE), 1)).astype(BF16)

    def update(zs, pvs, mask):
        c_run = c_ref[...]
        total = None
        for z, pv in zip(zs, pvs):
            ls = _log_sigmoid(z)
            lom = ls - z
            if mask is not None:
                lom = jnp.where(mask, lom, 0.0)
            hi, lo = _split2(lom)
            suffix = _dot(hi, later) + _dot(lo, later)
            a = jnp.exp(ls + suffix + c_run)
            if mask is not None:
                a = jnp.where(mask, a, 0.0)
            out = pv(a.astype(BF16))
            total = out if total is None else total + out
            c_run = c_run + jnp.sum(lom, axis=1, keepdims=True)
        acc_ref[...] += total
        c_ref[...] = c_run

    @pl.when(p_id == 0)
    def _():
        if first:
            c_ref[...] = jnp.zeros_like(c_ref)
            acc_ref[...] = jnp.zeros_like(acc_ref)
            i = _iota((LANES, PAGE), 0) % 16
            j = _iota((LANES, PAGE), 1)
            kv = new_ref[...]
            update([_dot_nt(q, kv[:, :width].astype(BF16))],
                   [lambda a: _dot(a, kv[:, width:].astype(BF16))], (j < i) & (j < 8))
        else:
            c_ref[...] = c_in_ref[:, 0:1]
            acc_ref[...] = acc_in_ref[...]

    @pl.when(p_id < lim_ref[pl.program_id(0)])
    def _():
        update([_dot(q, pg[0].astype(BF16)) for pg in pages],
               [lambda a, pg=pg: _dot_nt(a, pg[1].astype(BF16)) for pg in pages], None)

    @pl.when(p_id == pl.num_programs(1) - 1)
    def _():
        acc = acc_ref[...]
        col_head = _iota((16, width), 1) // D_B
        o = jnp.zeros((16, width), F32)
        for h in range(H_B):
            o = o + jnp.where(col_head == h, acc[h * 16:(h + 1) * 16, :], 0.0)
        o_ref[...] = o
        c_out_ref[...] = jnp.broadcast_to(c_ref[...], c_out_ref.shape)
        acc_out_ref[...] = acc


def _decode_stick(page_table, q_bd, new_kv, cache, layer, *, name):
    nb, n_pages = page_table.shape
    rows, width = q_bd.shape[1], q_bd.shape[2]
    n_steps = n_pages // PAGES_PER_STEP

    def call(first, steps, step_off, lim, c_in, acc_in, call_name):
        def page_spec(j):
            def idx(b, p, pt, lim_ref):
                p_eff = jnp.minimum(p, jnp.maximum(lim_ref[b] - 1, 0))
                return (layer, pt[b, n_pages - 1 - ((step_off + p_eff) * PAGES_PER_STEP + j)], 0, 0, 0)
            return pl.BlockSpec((None, None) + cache.shape[2:], idx)

        per_seq = lambda a: pl.BlockSpec((None,) + a.shape[1:], lambda b, p, pt, lim_ref: (b, 0, 0))
        out_shapes = [jax.ShapeDtypeStruct((nb, 16, width), F32), jax.ShapeDtypeStruct((nb, rows, LANES), F32),
                      jax.ShapeDtypeStruct((nb, rows, width), F32)]
        grid_spec = pltpu.PrefetchScalarGridSpec(
            num_scalar_prefetch=2, grid=(nb, steps),
            in_specs=[per_seq(q_bd), per_seq(new_kv), per_seq(c_in), per_seq(acc_in)]
            + [page_spec(j) for j in range(PAGES_PER_STEP)],
            out_specs=[per_seq(s) for s in out_shapes],
            scratch_shapes=[pltpu.VMEM((rows, 1), F32), pltpu.VMEM((rows, width), F32)])
        return pl.pallas_call(
            functools.partial(_dec_sb_body, first=first), grid_spec=grid_spec, out_shape=out_shapes,
            compiler_params=_params("parallel", "arbitrary"), name=call_name)(
                page_table, lim, q_bd, new_kv, c_in, acc_in, *([cache] * PAGES_PER_STEP))

    o, c_run, acc = call(True, 1, 0, jnp.ones((nb,), jnp.int32),
                         jnp.zeros((nb, rows, LANES), F32), jnp.zeros((nb, rows, width), F32), name + "_head")
    if n_steps == 1:
        return o
    live = jnp.max(c_run[:, :, 0], axis=1) > SB_CUTOFF
    lim = jnp.where(live, n_steps - 1, 0).astype(jnp.int32)
    return lax.cond(jnp.any(live),
                    lambda: call(False, n_steps - 1, 1, lim, c_run, acc, name + "_tail")[0],
                    lambda: o)


def _dec_fox_body(pt_ref, q_ref, new_ref, lfnew_ref, *rest):
    pages = rest[:PAGES_PER_STEP]
    lfs = rest[PAGES_PER_STEP:2 * PAGES_PER_STEP]
    o_ref, m_ref, l_ref, acc_ref, d_ref = rest[2 * PAGES_PER_STEP:]
    p_id = pl.program_id(1)
    q = q_ref[...]
    width = H_C * D_C
    rows = _iota((PAGE, PAGE), 0)
    cols = _iota((PAGE, PAGE), 1)
    later = (rows > cols).astype(BF16)
    upto = (rows <= cols).astype(BF16)
    expand = ((cols < 3 * H_C) & (cols % H_C == rows // 8)).astype(BF16)

    def head_table(lf_t, tri):
        parts = jnp.concatenate(_split3(lf_t), axis=0)
        d3 = _dot(parts, tri)
        return d3[0:H_C] + d3[H_C:2 * H_C] + d3[2 * H_C:3 * H_C]

    def to_rows(tab):
        parts = jnp.concatenate(_split3(tab) + (jnp.zeros((PAGE - 3 * H_C, PAGE), BF16),), axis=0)
        return _dot(expand, parts)

    @pl.when(p_id == 0)
    def _():
        m_ref[...] = jnp.full_like(m_ref, NEG)
        l_ref[...] = jnp.zeros_like(l_ref)
        acc_ref[...] = jnp.zeros_like(acc_ref)
        d_ref[...] = jnp.zeros_like(d_ref)
        i = rows % 8
        kv = new_ref[...]
        s = _dot_nt(q, kv[:, :width].astype(BF16)) - to_rows(head_table(lfnew_ref[...], upto))
        s = jnp.where((cols <= i) & (cols < 8), s, NEG)
        _softmax_update(s, lambda p: _dot(p, kv[:, width:].astype(BF16)), m_ref, l_ref, acc_ref)

    d_run = d_ref[...]
    scores = []
    for pg, lf in zip(pages, lfs):
        lf_t = lf[...]
        scores.append(_dot(q, pg[0].astype(BF16)) + to_rows(head_table(lf_t, later) + d_run))
        d_run = d_run + jnp.sum(lf_t, axis=1, keepdims=True)
    d_ref[...] = d_run

    def pv(p):
        out = _dot_nt(p[:, :PAGE], pages[0][1].astype(BF16))
        for j in range(1, PAGES_PER_STEP):
            out = out + _dot_nt(p[:, j * PAGE:(j + 1) * PAGE], pages[j][1].astype(BF16))
        return out

    _softmax_update(jnp.concatenate(scores, axis=1), pv, m_ref, l_ref, acc_ref)

    @pl.when(p_id == pl.num_programs(1) - 1)
    def _():
        o_all = acc_ref[...] / l_ref[...]
        col_head = _iota((8, width), 1) // D_C
        o = jnp.zeros((8, width), F32)
        for h in range(H_C):
            o = o + jnp.where(col_head == h, o_all[h * 8:(h + 1) * 8, :], 0.0)
        o_ref[...] = o


def _decode_attn(body, page_table, q_bd, new_kv, cache, layer, *, out_rows, out_cols, scratch,
                 reverse, extra=(), extra_specs=(), lf_new=None, lf_cache=None, name):
    nb, n_pages = page_table.shape
    steps = n_pages // PAGES_PER_STEP
    in_specs = list(extra_specs) + [
        pl.BlockSpec((None,) + q_bd.shape[1:], lambda b, p, pt: (b, 0, 0)),
        pl.BlockSpec((None,) + new_kv.shape[1:], lambda b, p, pt: (b, 0, 0))]
    args = list(extra) + [q_bd, new_kv]
    if lf_new is not None:
        in_specs.append(pl.BlockSpec((None,) + lf_new.shape[1:], lambda b, p, pt: (b, 0, 0)))
        args.append(lf_new)
    in_specs += _page_specs(cache.shape[2:], layer, n_pages, reverse)
    args += [cache] * PAGES_PER_STEP
    if lf_cache is not None:
        in_specs += _page_specs((H_C, PAGE), layer, n_pages, reverse)
        args += [lf_cache] * PAGES_PER_STEP
    grid_spec = pltpu.PrefetchScalarGridSpec(
        num_scalar_prefetch=1, grid=(nb, steps), in_specs=in_specs,
        out_specs=pl.BlockSpec((None, out_rows, out_cols), lambda b, p, pt: (b, 0, 0)),
        scratch_shapes=scratch)
    return pl.pallas_call(
        body, grid_spec=grid_spec,
        out_shape=jax.ShapeDtypeStruct((nb, out_rows, out_cols), F32),
        compiler_params=_params("parallel", "arbitrary"), name=name)(page_table, *args)


def _mem_body(q_ref, kv_ref, o_ref, kvb_ref, *, d_m, tiled_rows):
    hd = H_M * d_m

    @pl.when(pl.program_id(1) == 0)
    def _():
        if not tiled_rows:
            kvb_ref[...] = kv_ref[...].astype(BF16)
            return
        n_mem = kvb_ref.shape[0]
        halves = d_m // LANES
        per_tok = 2 * halves * H_M
        for kv in range(2):
            for dt in range(halves):
                for h in range(H_M):
                    col = kv * hd + h * d_m + dt * LANES
                    row = (kv * halves + dt) * H_M + h
                    kvb_ref[:, col:col + LANES] = kv_ref[pl.ds(row, n_mem, stride=per_tok), :].astype(BF16)

    for h in range(H_M):
        q = (q_ref[:, h * d_m:(h + 1) * d_m] * (d_m ** -0.5)).astype(BF16)
        s = _dot_nt(q, kvb_ref[:, h * d_m:(h + 1) * d_m])
        p = jnp.exp(s - jnp.max(s, axis=1, keepdims=True))
        o = _dot(p.astype(BF16), kvb_ref[:, hd + h * d_m: hd + (h + 1) * d_m])
        o_ref[:, h * d_m:(h + 1) * d_m] = (o / jnp.sum(p, axis=1, keepdims=True)).astype(o_ref.dtype)


def _mem_attn(q, mem_kv, kv_lead, *, nb, tq, n_mem, name):
    M, D = q.shape
    nq = M // nb // tq
    lead = tuple(kv_lead)
    return pl.pallas_call(
        functools.partial(_mem_body, d_m=D // H_M, tiled_rows=mem_kv.shape[-1] == LANES), grid=(nb, nq),
        in_specs=[pl.BlockSpec((tq, D), lambda b, i: (b * nq + i, 0)),
                  pl.BlockSpec((None,) * (len(lead) + 1) + mem_kv.shape[-2:], lambda b, i: lead + (b, 0, 0))],
        out_specs=pl.BlockSpec((tq, D), lambda b, i: (b * nq + i, 0)),
        out_shape=jax.ShapeDtypeStruct((M, D), BF16),
        scratch_shapes=[pltpu.VMEM((n_mem, 2 * D), BF16)],
        compiler_params=_params("parallel", "arbitrary"), name=name)(q, mem_kv)


def _mm_ln_body(*refs, n_in, alpha):
    xs = refs[:n_in]
    ws = refs[n_in:2 * n_in]
    r_ref, g_ref, b_ref, o_ref = refs[2 * n_in:]
    y = _dot(xs[0][...].astype(BF16), ws[0][...])
    for x_ref, w_ref in zip(xs[1:], ws[1:]):
        y = y + _dot(x_ref[...].astype(BF16), w_ref[...])
    o_ref[...] = _layer_norm(alpha * r_ref[...] + y, g_ref[...], b_ref[...])


def _mm_ln(xs, w, layer, res, g, b, *, tm, alpha, name):
    M, D = res.shape
    in_specs, args, off = [], [], 0
    for x in xs:
        in_specs.append(pl.BlockSpec((tm, x.shape[1]), lambda m: (m, 0)))
    for x in xs:
        kx = x.shape[1]
        in_specs.append(pl.BlockSpec((None, kx, D), lambda m, o=off // kx: (layer, o, 0)))
        off += kx
    in_specs += [pl.BlockSpec((tm, D), lambda m: (m, 0)),
                 pl.BlockSpec((1, D), lambda m: (0, 0)), pl.BlockSpec((1, D), lambda m: (0, 0))]
    return pl.pallas_call(
        functools.partial(_mm_ln_body, n_in=len(xs), alpha=alpha), grid=(M // tm,),
        in_specs=in_specs, out_specs=pl.BlockSpec((tm, D), lambda m: (m, 0)),
        out_shape=jax.ShapeDtypeStruct((M, D), F32),
        compiler_params=_params("parallel"), name=name)(*xs, *([w] * len(xs)), res, g, b)


def _swiglu_chunk(xb, wg_ref, wu_ref, wd_ref):
    hg = _dot(xb, wg_ref[...])
    hu = _dot(xb, wu_ref[...])
    a = hg * jax.nn.sigmoid(hg) * hu
    return _dot(a.astype(BF16), wd_ref[...])


def _ffn_body(x_ref, wg_ref, wu_ref, wd_ref, g_ref, b_ref, o_ref, xb_ref, acc_ref, *, alpha):
    f = pl.program_id(1)

    @pl.when(f == 0)
    def _():
        xb_ref[...] = x_ref[...].astype(BF16)
        acc_ref[...] = jnp.zeros_like(acc_ref)

    acc_ref[...] += _swiglu_chunk(xb_ref[...], wg_ref, wu_ref, wd_ref)

    @pl.when(f == pl.num_programs(1) - 1)
    def _():
        o_ref[...] = _layer_norm(alpha * x_ref[...] + acc_ref[...], g_ref[...], b_ref[...])


def _ffn(x, w_gu, w_down, layer, g, b, *, tm, tf, alpha, name):
    M, D = x.shape
    nf = w_down.shape[1] // tf
    return pl.pallas_call(
        functools.partial(_ffn_body, alpha=alpha), grid=(M // tm, nf),
        in_specs=[pl.BlockSpec((tm, D), lambda m, f: (m, 0)),
                  pl.BlockSpec((None, D, tf), lambda m, f: (layer, 0, f)),
                  pl.BlockSpec((None, D, tf), lambda m, f: (layer, 0, nf + f)),
                  pl.BlockSpec((None, tf, D), lambda m, f: (layer, f, 0)),
                  pl.BlockSpec((1, D), lambda m, f: (0, 0)), pl.BlockSpec((1, D), lambda m, f: (0, 0))],
        out_specs=pl.BlockSpec((tm, D), lambda m, f: (m, 0)),
        out_shape=jax.ShapeDtypeStruct((M, D), F32),
        scratch_shapes=[pltpu.VMEM((tm, D), BF16), pltpu.VMEM((tm, D), F32)],
        compiler_params=_params("parallel", "arbitrary"), name=name)(x, w_gu, w_gu, w_down, g, b)


def _moe_body(x_ref, wr_ref, wg_ref, wu_ref, wd_ref, g_ref, b_ref, o_ref,
              comb_ref, rank_ref, comb_t_ref, rank_t_ref, xc_ref, yc_ref, acc_ref, cnt_ref, *, alpha):
    e = pl.program_id(1)
    f = pl.program_id(2)
    last_f = f == pl.num_programs(2) - 1
    T = x_ref.shape[0]

    @pl.when((e == 0) & (f == 0))
    def _():
        x = x_ref[...]
        acc_ref[...] = jnp.zeros_like(acc_ref)
        logits = jnp.dot(x, wr_ref[...], precision=lax.Precision.HIGHEST, preferred_element_type=F32)
        lane = _iota(logits.shape, 1).astype(F32)
        logits = jnp.where(lane < N_EXPERTS, logits, NEG)
        m1 = jnp.max(logits, axis=1, keepdims=True)
        i1 = jnp.min(jnp.where(logits == m1, lane, float(LANES)), axis=1, keepdims=True)
        rest = jnp.where(lane == i1, NEG, logits)
        m2 = jnp.max(rest, axis=1, keepdims=True)
        i2 = jnp.min(jnp.where(rest == m2, lane, float(LANES)), axis=1, keepdims=True)
        e2 = jnp.exp(m2 - m1)
        den = 1.0 + e2
        comb = jnp.where(lane == i1, 1.0 / den, 0.0) + jnp.where(lane == i2, e2 / den, 0.0)
        comb_ref[...] = comb
        comb_t = comb.T
        comb_t_ref[...] = comb_t
        r = _iota((T, T), 0)
        c = _iota((T, T), 1)
        rank_ref[...] = _dot((c < r).astype(BF16), (comb > 0.0).astype(BF16))
        rank_t_ref[...] = _dot((comb_t > 0.0).astype(BF16), (r < c).astype(BF16))

    sub = _iota((LANES, T), 0)

    @pl.when(f == 0)
    def _():
        comb_t = comb_t_ref[...]
        routed_t = jnp.sum(jnp.where((sub == e) & (comb_t > 0.0), 1.0, 0.0), axis=0, keepdims=True)
        rank_t = jnp.sum(jnp.where(sub == e, rank_t_ref[...], 0.0), axis=0, keepdims=True)
        n_rows = jnp.sum(routed_t).astype(jnp.int32)
        cnt_ref[0] = (n_rows + MOE_CHUNK - 1) // MOE_CHUNK
        xb = x_ref[...].astype(BF16)
        slot = _iota((MOE_CHUNK, T), 0).astype(F32)

        def compact(ci, carry):
            base = pl.multiple_of(ci * MOE_CHUNK, MOE_CHUNK)
            pick = ((rank_t == slot + base.astype(F32)) & (routed_t > 0.0)).astype(BF16)
            xc_ref[pl.ds(base, MOE_CHUNK), :] = _dot(pick, xb).astype(BF16)
            yc_ref[pl.ds(base, MOE_CHUNK), :] = jnp.zeros((MOE_CHUNK, yc_ref.shape[1]), F32)
            return carry

        lax.fori_loop(0, cnt_ref[0], compact, 0)

    def expert(ci, carry):
        base = pl.multiple_of(ci * MOE_CHUNK, MOE_CHUNK)
        yc_ref[pl.ds(base, MOE_CHUNK), :] += _swiglu_chunk(xc_ref[pl.ds(base, MOE_CHUNK), :], wg_ref, wu_ref, wd_ref)
        return carry

    lax.fori_loop(0, cnt_ref[0], expert, 0)

    @pl.when(last_f)
    def _():
        comb = comb_ref[...]
        lane = _iota(comb.shape, 1)
        gate = jnp.sum(jnp.where(lane == e, comb, 0.0), axis=1, keepdims=True)
        rank = jnp.sum(jnp.where(lane == e, rank_ref[...], 0.0), axis=1, keepdims=True)
        slot = _iota((T, MOE_CHUNK), 1).astype(F32)

        def scatter(ci, carry):
            base = pl.multiple_of(ci * MOE_CHUNK, MOE_CHUNK)
            place = ((rank == slot + base.astype(F32)) & (gate > 0.0)).astype(BF16)
            acc_ref[...] += gate * _dot(place, yc_ref[pl.ds(base, MOE_CHUNK), :].astype(BF16))
            return carry

        lax.fori_loop(0, cnt_ref[0], scatter, 0)

    @pl.when(last_f & (e == pl.num_programs(1) - 1))
    def _():
        o_ref[...] = _layer_norm(alpha * x_ref[...] + acc_ref[...], g_ref[...], b_ref[...])


def _moe(x, w_router, w_gu, w_down, layer, g, b, *, tm, tf, alpha, name):
    M, D = x.shape
    nf = w_down.shape[2] // tf
    assert tm % MOE_CHUNK == 0 and M % tm == 0
    once = pl.Buffered(1)
    return pl.pallas_call(
        functools.partial(_moe_body, alpha=alpha), grid=(M // tm, N_EXPERTS, nf),
        in_specs=[pl.BlockSpec((tm, D), lambda m, e, f: (m, 0), pipeline_mode=once),
                  pl.BlockSpec((None, D, LANES), lambda m, e, f: (layer, 0, 0), pipeline_mode=once),
                  pl.BlockSpec((None, None, D, tf), lambda m, e, f: (layer, e, 0, f)),
                  pl.BlockSpec((None, None, D, tf), lambda m, e, f: (layer, e, 0, nf + f)),
                  pl.BlockSpec((None, None, tf, D), lambda m, e, f: (layer, e, f, 0)),
                  pl.BlockSpec((1, D), lambda m, e, f: (0, 0)), pl.BlockSpec((1, D), lambda m, e, f: (0, 0))],
        out_specs=pl.BlockSpec((tm, D), lambda m, e, f: (m, 0)),
        out_shape=jax.ShapeDtypeStruct((M, D), F32),
        scratch_shapes=[pltpu.VMEM((tm, LANES), F32), pltpu.VMEM((tm, LANES), F32),
                        pltpu.VMEM((LANES, tm), F32), pltpu.VMEM((LANES, tm), F32),
                        pltpu.VMEM((tm, D), BF16), pltpu.VMEM((tm, D), F32), pltpu.VMEM((tm, D), F32),
                        pltpu.SMEM((1,), jnp.int32)],
        compiler_params=_params("parallel", "arbitrary", "arbitrary"), name=name)(
            x, w_router, w_gu, w_gu, w_down, g, b)


def _rope_tables(pos):
    half = DK_A // 2
    inv = 1.0 / (ROPE_THETA ** (jnp.arange(half, dtype=F32) / half))
    ang = pos.astype(F32)[:, None] * inv[None, :]
    cos, sin = jnp.cos(ang), jnp.sin(ang)
    return jnp.concatenate([cos] * 4, axis=1), jnp.concatenate([-sin, sin, -sin, sin], axis=1)


def _block_diag_queries(q, nb, n_heads, n_maps, d, rows_per_head, scale):
    g = n_heads * n_maps
    lq = q.shape[0] // nb
    q5 = (q * scale).reshape(nb, lq, g, d)
    q5 = jnp.pad(q5, ((0, 0), (0, rows_per_head - lq), (0, 0), (0, 0)))
    eye = jnp.eye(g, dtype=q.dtype)
    out = jnp.einsum('bigd,gx->bgixd', q5, eye)
    return out.reshape(nb, g * rows_per_head, g * d).astype(BF16)


def _pad_rows(x, nb, rows):
    lq = x.shape[0] // nb
    return jnp.pad(x.reshape(nb, lq, x.shape[1]), ((0, 0), (0, rows - lq), (0, 0)))


def kernel(x_prompt, x_sample, cache_a_kv, cache_b_kv, cache_c_kv, cache_c_logf, cache_mem_kv,
           page_table, mem_prompt, even_w_in, even_w_out, diff_lambda, diff_subln_g,
           odd_w_in, odd_b_f, odd_w_out, mem_w_q, mem_w_kv, mem_w_o, ffn_w_gu, ffn_w_down,
           moe_w_router, moe_w_gu, moe_w_down, ln_g, ln_b):
    depth = ln_g.shape[0]
    alpha = (2 * depth) ** 0.25
    nbp, L, D = x_prompt.shape
    nbs, Ls, _ = x_sample.shape
    n_pages = page_table.shape[1]
    n_pool = cache_a_kv.shape[1]
    past_len = n_pages * PAGE
    n_mem = mem_prompt.shape[1]
    assert Ls == 8 and n_pages % PAGES_PER_STEP == 0
    qa_w = H_A * 2 * DK_A
    qb_w = H_B * D_B
    qc_w = H_C * D_C
    even_in = 3 * qa_w + 3 * qb_w
    odd_main = 3 * qc_w

    w_even_in = even_w_in.astype(BF16)
    w_even_out = even_w_out.astype(BF16)
    w_odd_in = odd_w_in[:, :, :odd_main].astype(BF16)
    w_odd_f = jnp.pad(odd_w_in[:, :, odd_main:], ((0, 0), (0, 0), (0, LANES - H_C))).astype(BF16)
    b_odd_f = jnp.pad(odd_b_f, ((0, 0), (0, LANES - H_C)))[:, None, :]
    w_odd_out = odd_w_out.astype(BF16)
    w_mem_q = mem_w_q.astype(BF16)
    w_mem_kv = mem_w_kv.astype(BF16)
    w_mem_o = mem_w_o.astype(BF16)
    w_ffn_gu = ffn_w_gu.astype(BF16)
    w_ffn_down = ffn_w_down.astype(BF16)
    w_moe_gu = moe_w_gu.astype(BF16)
    w_moe_down = moe_w_down.astype(BF16)
    w_router = jnp.pad(moe_w_router, ((0, 0), (0, 0), (0, LANES - N_EXPERTS)))

    ca = cache_a_kv.reshape(cache_a_kv.shape[0], n_pool, PAGE * 2 * H_A, 2 * DK_A)
    cb = jnp.transpose(cache_b_kv, (0, 1, 3, 4, 5, 2)).reshape(cache_b_kv.shape[0], n_pool, 2, qb_w, PAGE)
    cc = jnp.transpose(cache_c_kv, (0, 1, 3, 4, 5, 2)).reshape(cache_c_kv.shape[0], n_pool, 2, qc_w, PAGE)
    clf = jnp.swapaxes(cache_c_logf, 2, 3)
    d_m = D // H_M
    cmem = cache_mem_kv.reshape(depth, nbs, n_mem, 2, H_M, d_m // LANES, LANES)
    cmem = jnp.swapaxes(cmem, 4, 5).reshape(depth, nbs, n_mem * 2 * d_m // LANES * H_M, LANES)

    rope_p = _rope_tables(jnp.arange(L))
    rope_s = _rope_tables(past_len + (jnp.arange(nbs * Ls) % Ls))

    xp = x_prompt.reshape(nbp * L, D)
    xs = x_sample.reshape(nbs * Ls, D)
    mp = mem_prompt.reshape(nbp * n_mem, D)
    Mp, Ms = xp.shape[0], xs.shape[0]
    tmp, tms = min(1024, Mp), Ms
    tq = min(512, L)
    tk = 2 * tq if (L // tq) % 2 == 0 else tq
    tq_sb = min(256, L)

    a_p, a_s, b_p, b_s, c_p, c_s, lf_p, lf_s, mem_rows = [], [], [], [], [], [], [], [], []
    for l in range(depth):
        i = l // 2
        g3 = ln_g[l][:, None, :]
        b3 = ln_b[l][:, None, :]
        mem_kv_p = _proj(mp, w_mem_kv, l, 2 * D, tm=min(1024, mp.shape[0]), name=f"memkv{l}")
        mem_rows.append(mem_kv_p.reshape(nbp, n_mem, 2, H_M, D // H_M))
        mem_kv_p = mem_kv_p.reshape(nbp, n_mem, 2 * D)
        if l % 2 == 0:
            lam_init = _lambda_init(l)
            lv = diff_lambda[i]
            sg = diff_subln_g[i][None, :]
            hp = _proj(xp, w_even_in, i, even_in, tm=tmp, rope=rope_p, rope_cols=2 * qa_w, name=f"even_in_p{l}")
            a_p.append(hp[:, qa_w:3 * qa_w].reshape(nbp, L, 2, H_A, 2 * DK_A))
            b_p.append(hp[:, 3 * qa_w + qb_w:].reshape(nbp, L, 2, H_B, D_B))
            oa = _prompt_attn(functools.partial(_diff_body, tq=tq, tk=tk, lam_init=lam_init), hp, nb=nbp, L=L,
                              n_groups=H_A, qcol=0, kcol=H_A, vcol=2 * H_A, tq=tq, scratch=(), extra=(lv, sg),
                              extra_specs=(pl.BlockSpec(lv.shape, lambda b, g, i: (0, 0)),
                                           pl.BlockSpec(sg.shape, lambda b, g, i: (0, 0))),
                              name=f"diff_p{l}")
            ob = _prompt_attn(functools.partial(_sb_body, tq=tq_sb), hp, nb=nbp, L=L, n_groups=H_B // 2,
                              qcol=3 * H_A, kcol=3 * H_A + H_B // 2, vcol=3 * H_A + H_B, tq=tq_sb, scratch=(),
                              name=f"stick_p{l}")
            xp = _mm_ln([oa, ob], w_even_out, i, xp, g3[0], b3[0], tm=tmp, alpha=alpha, name=f"even_out_p{l}")
            hs = _proj(xs, w_even_in, i, even_in, tm=tms, rope=rope_s, rope_cols=2 * qa_w, name=f"even_in_s{l}")
            a_new = hs[:, qa_w:3 * qa_w]
            b_new = hs[:, 3 * qa_w + qb_w:]
            a_s.append(a_new.reshape(nbs, Ls, 2, H_A, 2 * DK_A))
            b_s.append(b_new.reshape(nbs, Ls, 2, H_B, D_B))
            qa_bd = _block_diag_queries(hs[:, :qa_w], nbs, H_A, 2, DK_A, 16, DK_A ** -0.5)
            qb_bd = _block_diag_queries(hs[:, 3 * qa_w:3 * qa_w + qb_w], nbs, H_B, 1, D_B, 16, D_B ** -0.5)
            oa = _decode_attn(functools.partial(_dec_diff_body, lam_init=lam_init), page_table, qa_bd,
                              _pad_rows(a_new, nbs, PAGE), ca, i, out_rows=16, out_cols=qa_w,
                              scratch=[pltpu.VMEM((LANES, 1), F32), pltpu.VMEM((LANES, 1), F32),
                                       pltpu.VMEM((LANES, qa_w), F32)],
                              reverse=False, extra=(lv, sg),
                              extra_specs=(pl.BlockSpec(lv.shape, lambda b, p, pt: (0, 0)),
                                           pl.BlockSpec(sg.shape, lambda b, p, pt: (0, 0))),
                              name=f"diff_s{l}")
            ob = _decode_stick(page_table, qb_bd, _pad_rows(b_new, nbs, PAGE), cb, i, name=f"stick_s{l}")
            oa = oa[:, :Ls].reshape(Ms, qa_w)
            ob = ob[:, :Ls].reshape(Ms, qb_w)
            xs = _mm_ln([oa, ob], w_even_out, i, xs, g3[0], b3[0], tm=tms, alpha=alpha, name=f"even_out_s{l}")
        else:
            hp = _proj(xp, w_odd_in, i, odd_main, tm=tmp, name=f"odd_in_p{l}")
            c_p.append(hp[:, qc_w:].reshape(nbp, L, 2, H_C, D_C))
            lf, cs = _logf(xp, w_odd_f[i], b_odd_f[i], nb=nbp, tl=tq_sb, with_cumsum=True, name=f"logf_p{l}")
            lf_p.append(lf[:, :H_C].reshape(nbp, L, H_C))
            nq = L // tq
            oc = _prompt_attn(functools.partial(_fox_body, tq=tq, tk=tk), hp, nb=nbp, L=L, n_groups=H_C // 2,
                              qcol=0, kcol=H_C // 2, vcol=H_C, tq=tq, scratch=(2,), extra=(cs, cs),
                              extra_specs=(pl.BlockSpec((tq, LANES), lambda b, g, i: (b * nq + i, 0)),
                                           pl.BlockSpec((L, LANES), lambda b, g, i: (b, 0))),
                              name=f"fox_p{l}")
            xp = _mm_ln([oc], w_odd_out, i, xp, g3[0], b3[0], tm=tmp, alpha=alpha, name=f"odd_out_p{l}")
            hs = _proj(xs, w_odd_in, i, odd_main, tm=tms, name=f"odd_in_s{l}")
            c_new = hs[:, qc_w:]
            c_s.append(c_new.reshape(nbs, Ls, 2, H_C, D_C))
            lf, _ = _logf(xs, w_odd_f[i], b_odd_f[i], nb=1, tl=Ms, with_cumsum=False, name=f"logf_s{l}")
            lf_s.append(lf[:, :H_C].reshape(nbs, Ls, H_C))
            lf_new_t = jnp.swapaxes(_pad_rows(lf[:, :H_C], nbs, PAGE), 1, 2)
            qc_bd = _block_diag_queries(hs[:, :qc_w], nbs, H_C, 1, D_C, 8, D_C ** -0.5)
            oc = _decode_attn(_dec_fox_body, page_table, qc_bd, _pad_rows(c_new, nbs, PAGE), cc, i,
                              out_rows=8, out_cols=qc_w,
                              scratch=[pltpu.VMEM((LANES, 1), F32), pltpu.VMEM((LANES, 1), F32),
                                       pltpu.VMEM((LANES, qc_w), F32), pltpu.VMEM((H_C, 1), F32)],
                              reverse=True, lf_new=lf_new_t, lf_cache=clf, name=f"fox_s{l}")
            oc = oc.reshape(Ms, qc_w)
            xs = _mm_ln([oc], w_odd_out, i, xs, g3[0], b3[0], tm=tms, alpha=alpha, name=f"odd_out_s{l}")

        qm = _proj(xp, w_mem_q, l, D, tm=tmp, name=f"mem_q_p{l}")
        om = _mem_attn(qm, mem_kv_p, (), nb=nbp, tq=tq, n_mem=n_mem, name=f"mem_attn_p{l}")
        xp = _mm_ln([om], w_mem_o, l, xp, g3[1], b3[1], tm=tmp, alpha=alpha, name=f"mem_out_p{l}")
        qm = _proj(xs, w_mem_q, l, D, tm=tms, name=f"mem_q_s{l}")
        om = _mem_attn(qm, cmem, (l,), nb=nbs, tq=Ls, n_mem=n_mem, name=f"mem_attn_s{l}")
        xs = _mm_ln([om], w_mem_o, l, xs, g3[1], b3[1], tm=tms, alpha=alpha, name=f"mem_out_s{l}")

        if l % 2 == 0:
            xp = _ffn(xp, w_ffn_gu, w_ffn_down, i, g3[2], b3[2], tm=min(1024, Mp), tf=256, alpha=alpha, name=f"ffn_p{l}")
            xs = _ffn(xs, w_ffn_gu, w_ffn_down, i, g3[2], b3[2], tm=tms, tf=256, alpha=alpha, name=f"ffn_s{l}")
        else:
            tf_moe = w_moe_down.shape[2] // 2
            xp = _moe(xp, w_router, w_moe_gu, w_moe_down, i, g3[2], b3[2], tm=min(1024, Mp), tf=tf_moe,
                      alpha=alpha, name=f"moe_p{l}")
            xs = _moe(xs, w_router, w_moe_gu, w_moe_down, i, g3[2], b3[2], tm=tms, tf=tf_moe, alpha=alpha,
                      name=f"moe_s{l}")

    return (xp.reshape(nbp, L, D), xs.reshape(nbs, Ls, D),
            jnp.stack(a_p), jnp.stack(a_s), jnp.stack(b_p), jnp.stack(b_s),
            jnp.stack(c_p), jnp.stack(c_s), jnp.stack(lf_p), jnp.stack(lf_s), jnp.stack(mem_rows))
```

```python
import functools
import math

import jax
import jax.numpy as jnp
from jax import lax
from jax.experimental import pallas as pl
from jax.experimental.pallas import tpu as pltpu

F32 = jnp.float32
BF16 = jnp.bfloat16

H_A, DK_A = 4, 64
H_B, D_B = 8, 64
H_C, D_C = 16, 64
H_M = 4
N_EXPERTS, TOP_K = 8, 2
ROPE_THETA = 10000.0
LN_EPS = 1e-5
RMS_EPS = 1e-5
PAGE = 128
LANES = 128
NEG = -1e30
SB_CUTOFF = -104.0
VMEM_LIMIT = 56 << 20
PAGES_PER_STEP = 8
MOE_CHUNK = 128


def _lambda_init(layer_idx):
    return 0.8 - 0.6 * math.exp(-0.3 * layer_idx)


def _params(*sem):
    return pltpu.CompilerParams(dimension_semantics=sem, vmem_limit_bytes=VMEM_LIMIT)


def _dot(a, b):
    return jnp.dot(a, b, preferred_element_type=F32)


def _dot_nt(a, b):
    return lax.dot_general(a, b, (((1,), (1,)), ((), ())), preferred_element_type=F32)


def _log_sigmoid(z):
    return jnp.minimum(z, 0.0) - jnp.log1p(jnp.exp(-jnp.abs(z)))


def _layer_norm(v, g, b):
    mu = jnp.mean(v, axis=1, keepdims=True)
    d = v - mu
    var = jnp.mean(d * d, axis=1, keepdims=True)
    return d * lax.rsqrt(var + LN_EPS) * g + b


def _split2(x):
    hi = x.astype(BF16)
    lo = (x - hi.astype(F32)).astype(BF16)
    return hi, lo


def _split3(x):
    h1 = x.astype(BF16)
    r1 = x - h1.astype(F32)
    h2 = r1.astype(BF16)
    h3 = (r1 - h2.astype(F32)).astype(BF16)
    return h1, h2, h3


def _iota(shape, dim):
    return lax.broadcasted_iota(jnp.int32, shape, dim)


def _proj_body(*refs, n_rope):
    if n_rope:
        x_ref, w_ref, cos_ref, sin_ref, o_ref = refs
    else:
        x_ref, w_ref, o_ref = refs
    h = _dot(x_ref[...].astype(BF16), w_ref[...])
    if not n_rope:
        o_ref[...] = h
        return
    n = pl.program_id(1)
    tn = h.shape[1]

    @pl.when(n < n_rope)
    def _():
        reps = tn // LANES
        c = jnp.concatenate([cos_ref[...]] * reps, axis=1)
        s = jnp.concatenate([sin_ref[...]] * reps, axis=1)
        lane = _iota(h.shape, 1)
        first = (lane % 64) < 32
        rot = jnp.where(first, pltpu.roll(h, tn - 32, 1), pltpu.roll(h, 32, 1))
        o_ref[...] = h * c + rot * s

    @pl.when(n >= n_rope)
    def _():
        o_ref[...] = h


def _proj(x, w, layer, n_cols, *, tm, rope=None, rope_cols=0, name):
    M, K = x.shape
    tn = 1024 if n_cols % 1024 == 0 and rope_cols % 1024 == 0 else 512
    n_rope = rope_cols // tn
    grid = (M // tm, n_cols // tn)
    in_specs = [pl.BlockSpec((tm, K), lambda m, n: (m, 0)),
                pl.BlockSpec((None, K, tn), lambda m, n: (layer, 0, n))]
    args = [x, w]
    if n_rope:
        cos, sin = rope
        nt = cos.shape[0] // tm
        in_specs += [pl.BlockSpec((tm, LANES), lambda m, n: (m % nt, 0)),
                     pl.BlockSpec((tm, LANES), lambda m, n: (m % nt, 0))]
        args += [cos, sin]
    return pl.pallas_call(
        functools.partial(_proj_body, n_rope=n_rope),
        grid=grid, in_specs=in_specs,
        out_specs=pl.BlockSpec((tm, tn), lambda m, n: (m, n)),
        out_shape=jax.ShapeDtypeStruct((M, n_cols), F32),
        compiler_params=_params("parallel", "arbitrary"), name=name)(*args)


def _logf_body(x_ref, w_ref, b_ref, lf_ref, cs_ref, carry_ref, *, with_cumsum):
    t = pl.program_id(1)

    @pl.when(t == 0)
    def _():
        carry_ref[...] = jnp.zeros_like(carry_ref)

    f = _dot(x_ref[...].astype(BF16), w_ref[...]) + b_ref[...]
    lf = _log_sigmoid(f)
    lf_ref[...] = lf
    tl = lf.shape[0]
    if not with_cumsum:
        cs_ref[...] = lf
        return
    tri = (_iota((tl, tl), 1) <= _iota((tl, tl), 0)).astype(BF16)
    parts = _split3(lf)
    cs = carry_ref[...] + _dot(tri, parts[0]) + _dot(tri, parts[1]) + _dot(tri, parts[2])
    cs_ref[...] = cs
    carry_ref[...] = cs[tl - 1:tl, :]


def _logf(x, wf, bf, *, nb, tl, with_cumsum, name):
    M, K = x.shape
    nt = M // nb // tl
    return pl.pallas_call(
        functools.partial(_logf_body, with_cumsum=with_cumsum), grid=(nb, nt),
        in_specs=[pl.BlockSpec((tl, K), lambda b, t: (b * nt + t, 0)),
                  pl.BlockSpec((K, LANES), lambda b, t: (0, 0)),
                  pl.BlockSpec((1, LANES), lambda b, t: (0, 0))],
        out_specs=[pl.BlockSpec((tl, LANES), lambda b, t: (b * nt + t, 0)),
                   pl.BlockSpec((tl, LANES), lambda b, t: (b * nt + t, 0))],
        out_shape=[jax.ShapeDtypeStruct((M, LANES), F32)] * 2,
        scratch_shapes=[pltpu.VMEM((1, LANES), F32)],
        compiler_params=_params("parallel", "arbitrary"), name=name)(x, wf, bf)


def _load_kv(k_ref, v_ref, kb_ref, vb_ref):
    @pl.when(pl.program_id(2) == 0)
    def _():
        kb_ref[...] = k_ref[...].astype(BF16)
        vb_ref[...] = v_ref[...].astype(BF16)


def _softmax_step(s, v, m, l, acc):
    m_new = jnp.maximum(m, jnp.max(s, axis=1, keepdims=True))
    alpha = jnp.exp(m - m_new)
    p = jnp.exp(s - m_new)
    l = alpha * l + jnp.sum(p, axis=1, keepdims=True)
    acc = alpha * acc + _dot(p.astype(BF16), v)
    return m_new, l, acc


def _causal_mask(q0, k0, tq, width):
    return k0 + _iota((tq, width), 1) <= q0 + _iota((tq, width), 0)


def _causal_sweep(step, qi, tq, tk, init):
    n_full = (qi * tq) // tk
    carry = lax.fori_loop(0, n_full, lambda i, c: step(pl.multiple_of(i * tk, tk), tk, c, False), init)
    off = pl.multiple_of(n_full * tk, tk)
    if tk == tq:
        return step(off, tq, carry, True)
    return lax.cond((qi * tq) % tk == 0,
                    lambda c: step(off, tq, c, True), lambda c: step(off, tk, c, True), carry)


def _diff_body(lv_ref, g_ref, q_ref, k_ref, v_ref, o_ref, kb_ref, vb_ref, *, tq, tk, lam_init):
    _load_kv(k_ref, v_ref, kb_ref, vb_ref)
    qi = pl.program_id(2)
    q = q_ref[...] * (DK_A ** -0.5)
    lane = _iota(q.shape, 1)
    qs = (jnp.where(lane < 64, q, 0.0).astype(BF16), jnp.where(lane >= 64, q, 0.0).astype(BF16))

    def step(off, width, carry, masked):
        k = kb_ref[pl.ds(off, width), :]
        v = vb_ref[pl.ds(off, width), :]
        out = []
        for c in range(2):
            s = _dot_nt(qs[c], k)
            if masked:
                s = jnp.where(_causal_mask(qi * tq, off, tq, width), s, NEG)
            out.append(_softmax_step(s, v, *carry[c]))
        return tuple(out)

    z1 = jnp.zeros((tq, 1), F32)
    init = ((z1 + NEG, z1, jnp.zeros((tq, LANES), F32)),) * 2
    (m0, l0, a0), (m1, l1, a1) = _causal_sweep(step, qi, tq, tk, init)
    lv = lv_ref[...]
    lam = (jnp.exp(jnp.sum(lv[0:1, :] * lv[1:2, :], axis=1, keepdims=True))
           - jnp.exp(jnp.sum(lv[2:3, :] * lv[3:4, :], axis=1, keepdims=True)) + lam_init)
    o = a0 / l0 - lam * (a1 / l1)
    o = o * lax.rsqrt(jnp.mean(o * o, axis=1, keepdims=True) + RMS_EPS) * g_ref[...]
    o_ref[...] = (o * (1.0 - lam_init)).astype(o_ref.dtype)


def _sb_body(q_ref, k_ref, v_ref, o_ref, kb_ref, vb_ref, *, tq):
    _load_kv(k_ref, v_ref, kb_ref, vb_ref)
    qi = pl.program_id(2)
    q = q_ref[...] * (D_B ** -0.5)
    lane = _iota(q.shape, 1)
    rows = _iota((tq, tq), 0)
    cols = _iota((tq, tq), 1)
    later = (rows > cols).astype(BF16)
    strict = cols < rows
    qs = [jnp.where((lane >= 64 * hh) & (lane < 64 * hh + 64), q, 0.0).astype(BF16) for hh in range(2)]

    def step(ki, carry, diag):
        off = pl.multiple_of(ki * tq, tq)
        k = kb_ref[pl.ds(off, tq), :]
        v = vb_ref[pl.ds(off, tq), :]
        out = []
        for hh in range(2):
            c_run, acc = carry[hh]
            z = _dot_nt(qs[hh], k)
            ls = _log_sigmoid(z)
            lom = ls - z
            if diag:
                lom = jnp.where(strict, lom, 0.0)
            hi, lo = _split2(lom)
            suffix = _dot(hi, later) + _dot(lo, later)
            a = jnp.exp(ls + suffix + c_run)
            if diag:
                a = jnp.where(strict, a, 0.0)
            out.append((c_run + jnp.sum(lom, axis=1, keepdims=True), acc + _dot(a.astype(BF16), v)))
        return tuple(out)

    def live(carry):
        return jnp.max(jnp.maximum(carry[0][0], carry[1][0])) > SB_CUTOFF

    init = ((jnp.zeros((tq, 1), F32), jnp.zeros((tq, LANES), F32)),) * 2
    carry = step(qi, init, True)
    _, carry = lax.while_loop(lambda st: (st[0] < qi) & live(st[1]),
                              lambda st: (st[0] + 1, step(qi - 1 - st[0], st[1], False)),
                              (jnp.int32(0), carry))
    o_ref[...] = jnp.where(lane < 64, carry[0][1], carry[1][1]).astype(o_ref.dtype)


def _fox_extra_lanes(lane, hh, parts, ones_first):
    e0 = 64 * (1 - hh)
    one = jnp.where((lane >= e0 + (0 if ones_first else 3)) & (lane < e0 + (3 if ones_first else 6)), 1.0, 0.0)
    p0 = e0 + (3 if ones_first else 0)
    return (one + jnp.where(lane == p0, parts[0], 0.0) + jnp.where(lane == p0 + 1, parts[1], 0.0)
            + jnp.where(lane == p0 + 2, parts[2], 0.0))


def _split3_f32(x):
    h1 = x.astype(BF16).astype(F32)
    r1 = x - h1
    h2 = r1.astype(BF16).astype(F32)
    return h1, h2, (r1 - h2).astype(BF16).astype(F32)


def _fox_body(fq_ref, fk_ref, q_ref, k_ref, v_ref, o_ref, ka_ref, va_ref, *, tq, tk):
    g = pl.program_id(1)
    qi = pl.program_id(2)

    @pl.when(qi == 0)
    def _():
        k = k_ref[...]
        v = v_ref[...]
        f_all = fk_ref[...]
        lane = _iota(k.shape, 1)
        for hh in range(2):
            own = (lane >= 64 * hh) & (lane < 64 * hh + 64)
            f = jnp.sum(jnp.where(lane == 2 * g + hh, f_all, 0.0), axis=1, keepdims=True)
            parts = [-p for p in _split3_f32(f)]
            ka_ref[hh] = jnp.where(own, k, _fox_extra_lanes(lane, hh, parts, True)).astype(BF16)
            va_ref[hh] = jnp.where(own, v, jnp.where(lane == 64 * (1 - hh), 1.0, 0.0)).astype(BF16)

    q = q_ref[...] * (D_C ** -0.5)
    lane = _iota(q.shape, 1)
    fq_all = fq_ref[...]
    qs = []
    for hh in range(2):
        own = (lane >= 64 * hh) & (lane < 64 * hh + 64)
        fq = jnp.sum(jnp.where(lane == 2 * g + hh, fq_all, 0.0), axis=1, keepdims=True)
        qs.append(jnp.where(own, q, _fox_extra_lanes(lane, hh, _split3_f32(fq), False)).astype(BF16))

    def step(off, width, carry, masked):
        out = []
        for hh in range(2):
            m, acc = carry[hh]
            s = _dot_nt(qs[hh], ka_ref[hh, pl.ds(off, width), :])
            if masked:
                s = jnp.where(_causal_mask(qi * tq, off, tq, width), s, NEG)
            m_new = jnp.maximum(m, jnp.max(s, axis=1, keepdims=True))
            p = jnp.exp(s - m_new).astype(BF16)
            out.append((m_new, jnp.exp(m - m_new) * acc + _dot(p, va_ref[hh, pl.ds(off, width), :])))
        return tuple(out)

    init = ((jnp.full((tq, 1), NEG, F32), jnp.zeros((tq, LANES), F32)),) * 2
    (_, a0), (_, a1) = _causal_sweep(step, qi, tq, tk, init)
    o_ref[...] = jnp.where(lane < 64, a0 / a0[:, 64:65], a1 / a1[:, 0:1]).astype(o_ref.dtype)


def _prompt_attn(body, h_all, *, nb, L, n_groups, qcol, kcol, vcol, tq, scratch, extra=(), extra_specs=(), name):
    nq = L // tq
    in_specs = list(extra_specs) + [
        pl.BlockSpec((tq, LANES), lambda b, g, i: (b * nq + i, qcol + g)),
        pl.BlockSpec((L, LANES), lambda b, g, i: (b, kcol + g)),
        pl.BlockSpec((L, LANES), lambda b, g, i: (b, vcol + g))]
    return pl.pallas_call(
        body, grid=(nb, n_groups, nq), in_specs=in_specs,
        out_specs=pl.BlockSpec((tq, LANES), lambda b, g, i: (b * nq + i, g)),
        out_shape=jax.ShapeDtypeStruct((nb * L, n_groups * LANES), BF16),
        scratch_shapes=[pltpu.VMEM(scratch + (L, LANES), BF16), pltpu.VMEM(scratch + (L, LANES), BF16)],
        compiler_params=_params("parallel", "parallel", "arbitrary"), name=name)(*extra, h_all, h_all, h_all)


def _page_specs(block, layer, n_pages, reverse):
    specs = []
    for j in range(PAGES_PER_STEP):
        def idx(b, p, pt, j=j):
            lp = p * PAGES_PER_STEP + j
            if reverse:
                lp = n_pages - 1 - lp
            return (layer, pt[b, lp]) + (0,) * len(block)
        specs.append(pl.BlockSpec((None, None) + block, idx))
    return specs


def _softmax_update(s, pv, m_ref, l_ref, acc_ref):
    m = m_ref[...]
    m_new = jnp.maximum(m, jnp.max(s, axis=1, keepdims=True))
    alpha = jnp.exp(m - m_new)
    p = jnp.exp(s - m_new)
    m_ref[...] = m_new
    l_ref[...] = alpha * l_ref[...] + jnp.sum(p, axis=1, keepdims=True)
    acc_ref[...] = alpha * acc_ref[...] + pv(p.astype(BF16))


def _dec_diff_body(pt_ref, lv_ref, g_ref, q_ref, new_ref, *rest, lam_init):
    pages = rest[:PAGES_PER_STEP]
    o_ref, m_ref, l_ref, acc_ref = rest[PAGES_PER_STEP:]
    p_id = pl.program_id(1)
    q = q_ref[...]
    width = H_A * 2 * DK_A

    def update(ks, vs, mask):
        s = jnp.concatenate([_dot_nt(q, k.astype(BF16)) for k in ks], axis=1)
        if mask is not None:
            s = jnp.where(mask, s, NEG)

        def pv(p):
            out = _dot(p[:, :PAGE], vs[0].astype(BF16))
            for j in range(1, len(vs)):
                out = out + _dot(p[:, j * PAGE:(j + 1) * PAGE], vs[j].astype(BF16))
            return out

        _softmax_update(s, pv, m_ref, l_ref, acc_ref)

    @pl.when(p_id == 0)
    def _():
        m_ref[...] = jnp.full_like(m_ref, NEG)
        l_ref[...] = jnp.zeros_like(l_ref)
        acc_ref[...] = jnp.zeros_like(acc_ref)
        i = _iota((LANES, PAGE), 0) % 16
        j = _iota((LANES, PAGE), 1)
        kv = new_ref[...]
        update([kv[:, :width]], [kv[:, width:]], (j <= i) & (j < 8))

    def head_rows(pg, kv):
        return jnp.concatenate([pg[pl.ds(kv * H_A + h, PAGE, stride=2 * H_A), :] for h in range(H_A)], axis=1)

    update([head_rows(pg, 0) for pg in pages], [head_rows(pg, 1) for pg in pages], None)

    @pl.when(p_id == pl.num_programs(1) - 1)
    def _():
        lv = lv_ref[...]
        lam = (jnp.exp(jnp.sum(lv[0:1, :] * lv[1:2, :], axis=1, keepdims=True))
               - jnp.exp(jnp.sum(lv[2:3, :] * lv[3:4, :], axis=1, keepdims=True)) + lam_init)
        o_all = acc_ref[...] / l_ref[...]
        for h in range(H_A):
            r0 = (2 * h) * 16
            cs = slice(h * LANES, (h + 1) * LANES)
            o = o_all[r0:r0 + 16, cs] - lam * o_all[r0 + 16:r0 + 32, cs]
            o = o * lax.rsqrt(jnp.mean(o * o, axis=1, keepdims=True) + RMS_EPS) * g_ref[...]
            o_ref[:, cs] = o * (1.0 - lam_init)


def _dec_sb_body(pt_ref, lim_ref, q_ref, new_ref, c_in_ref, acc_in_ref, *rest, first):
    pages = rest[:PAGES_PER_STEP]
    o_ref, c_out_ref, acc_out_ref, c_ref, acc_ref = rest[PAGES_PER_STEP:]
    p_id = pl.program_id(1)
    q = q_ref[...]
    width = H_B * D_B
    later = (_iota((PAGE, PAGE), 0) > _iota((PAGE, PAGE), 1)).astype(BF16)

    def update(zs, pvs, mask):
        c_run = c_ref[...]
        total = None
        for z, pv in zip(zs, pvs):
            ls = _log_sigmoid(z)
            lom = ls - z
            if mask is not None:
                lom = jnp.where(mask, lom, 0.0)
            hi, lo = _split2(lom)
            suffix = _dot(hi, later) + _dot(lo, later)
            a = jnp.exp(ls + suffix + c_run)
            if mask is not None:
                a = jnp.where(mask, a, 0.0)
            out = pv(a.astype(BF16))
            total = out if total is None else total + out
            c_run = c_run + jnp.sum(lom, axis=1, keepdims=True)
        acc_ref[...] += total
        c_ref[...] = c_run

    @pl.when(p_id == 0)
    def _():
        if first:
            c_ref[...] = jnp.zeros_like(c_ref)
            acc_ref[...] = jnp.zeros_like(acc_ref)
            i = _iota((LANES, PAGE), 0) % 16
            j = _iota((LANES, PAGE), 1)
            kv = new_ref[...]
            update([_dot_nt(q, kv[:, :width].astype(BF16))],
                   [lambda a: _dot(a, kv[:, width:].astype(BF16))], (j < i) & (j < 8))
        else:
            c_ref[...] = c_in_ref[:, 0:1]
            acc_ref[...] = acc_in_ref[...]

    @pl.when(p_id < lim_ref[pl.program_id(0)])
    def _():
        update([_dot(q, pg[0].astype(BF16)) for pg in pages],
               [lambda a, pg=pg: _dot_nt(a, pg[1].astype(BF16)) for pg in pages], None)

    @pl.when(p_id == pl.num_programs(1) - 1)
    def _():
        acc = acc_ref[...]
        col_head = _iota((16, width), 1) // D_B
        o = jnp.zeros((16, width), F32)
        for h in range(H_B):
            o = o + jnp.where(col_head == h, acc[h * 16:(h + 1) * 16, :], 0.0)
        o_ref[...] = o
        c_out_ref[...] = jnp.broadcast_to(c_ref[...], c_out_ref.shape)
        acc_out_ref[...] = acc


def _decode_stick(page_table, q_bd, new_kv, cache, layer, *, name):
    nb, n_pages = page_table.shape
    rows, width = q_bd.shape[1], q_bd.shape[2]
    n_steps = n_pages // PAGES_PER_STEP

    def call(first, steps, step_off, lim, c_in, acc_in, call_name):
        def page_spec(j):
            def idx(b, p, pt, lim_ref):
                p_eff = jnp.minimum(p, jnp.maximum(lim_ref[b] - 1, 0))
                return (layer, pt[b, n_pages - 1 - ((step_off + p_eff) * PAGES_PER_STEP + j)], 0, 0, 0)
            return pl.BlockSpec((None, None) + cache.shape[2:], idx)

        per_seq = lambda a: pl.BlockSpec((None,) + a.shape[1:], lambda b, p, pt, lim_ref: (b, 0, 0))
        out_shapes = [jax.ShapeDtypeStruct((nb, 16, width), F32), jax.ShapeDtypeStruct((nb, rows, LANES), F32),
                      jax.ShapeDtypeStruct((nb, rows, width), F32)]
        grid_spec = pltpu.PrefetchScalarGridSpec(
            num_scalar_prefetch=2, grid=(nb, steps),
            in_specs=[per_seq(q_bd), per_seq(new_kv), per_seq(c_in), per_seq(acc_in)]
            + [page_spec(j) for j in range(PAGES_PER_STEP)],
            out_specs=[per_seq(s) for s in out_shapes],
            scratch_shapes=[pltpu.VMEM((rows, 1), F32), pltpu.VMEM((rows, width), F32)])
        return pl.pallas_call(
            functools.partial(_dec_sb_body, first=first), grid_spec=grid_spec, out_shape=out_shapes,
            compiler_params=_params("parallel", "arbitrary"), name=call_name)(
                page_table, lim, q_bd, new_kv, c_in, acc_in, *([cache] * PAGES_PER_STEP))

    o, c_run, acc = call(True, 1, 0, jnp.ones((nb,), jnp.int32),
                         jnp.zeros((nb, rows, LANES), F32), jnp.zeros((nb, rows, width), F32), name + "_head")
    if n_steps == 1:
        return o
    live = jnp.max(c_run[:, :, 0], axis=1) > SB_CUTOFF
    lim = jnp.where(live, n_steps - 1, 0).astype(jnp.int32)
    return lax.cond(jnp.any(live),
                    lambda: call(False, n_steps - 1, 1, lim, c_run, acc, name + "_tail")[0],
                    lambda: o)


def _dec_fox_body(pt_ref, q_ref, new_ref, lfnew_ref, *rest):
    pages = rest[:PAGES_PER_STEP]
    lfs = rest[PAGES_PER_STEP:2 * PAGES_PER_STEP]
    o_ref, m_ref, l_ref, acc_ref, d_ref = rest[2 * PAGES_PER_STEP:]
    p_id = pl.program_id(1)
    q = q_ref[...]
    width = H_C * D_C
    rows = _iota((PAGE, PAGE), 0)
    cols = _iota((PAGE, PAGE), 1)
    later = (rows > cols).astype(BF16)
    upto = (rows <= cols).astype(BF16)
    expand = ((cols < 3 * H_C) & (cols % H_C == rows // 8)).astype(BF16)

    def head_table(lf_t, tri):
        parts = jnp.concatenate(_split3(lf_t), axis=0)
        d3 = _dot(parts, tri)
        return d3[0:H_C] + d3[H_C:2 * H_C] + d3[2 * H_C:3 * H_C]

    def to_rows(tab):
        parts = jnp.concatenate(_split3(tab) + (jnp.zeros((PAGE - 3 * H_C, PAGE), BF16),), axis=0)
        return _dot(expand, parts)

    @pl.when(p_id == 0)
    def _():
        m_ref[...] = jnp.full_like(m_ref, NEG)
        l_ref[...] = jnp.zeros_like(l_ref)
        acc_ref[...] = jnp.zeros_like(acc_ref)
        d_ref[...] = jnp.zeros_like(d_ref)
        i = rows % 8
        kv = new_ref[...]
        s = _dot_nt(q, kv[:, :width].astype(BF16)) - to_rows(head_table(lfnew_ref[...], upto))
        s = jnp.where((cols <= i) & (cols < 8), s, NEG)
        _softmax_update(s, lambda p: _dot(p, kv[:, width:].astype(BF16)), m_ref, l_ref, acc_ref)

    d_run = d_ref[...]
    scores = []
    for pg, lf in zip(pages, lfs):
        lf_t = lf[...]
        scores.append(_dot(q, pg[0].astype(BF16)) + to_rows(head_table(lf_t, later) + d_run))
        d_run = d_run + jnp.sum(lf_t, axis=1, keepdims=True)
    d_ref[...] = d_run

    def pv(p):
        out = _dot_nt(p[:, :PAGE], pages[0][1].astype(BF16))
        for j in range(1, PAGES_PER_STEP):
            out = out + _dot_nt(p[:, j * PAGE:(j + 1) * PAGE], pages[j][1].astype(BF16))
        return out

    _softmax_update(jnp.concatenate(scores, axis=1), pv, m_ref, l_ref, acc_ref)

    @pl.when(p_id == pl.num_programs(1) - 1)
    def _():
        o_all = acc_ref[...] / l_ref[...]
        col_head = _iota((8, width), 1) // D_C
        o = jnp.zeros((8, width), F32)
        for h in range(H_C):
            o = o + jnp.where(col_head == h, o_all[h * 8:(h + 1) * 8, :], 0.0)
        o_ref[...] = o


def _decode_attn(body, page_table, q_bd, new_kv, cache, layer, *, out_rows, out_cols, scratch,
                 reverse, extra=(), extra_specs=(), lf_new=None, lf_cache=None, name):
    nb, n_pages = page_table.shape
    steps = n_pages // PAGES_PER_STEP
    in_specs = list(extra_specs) + [
        pl.BlockSpec((None,) + q_bd.shape[1:], lambda b, p, pt: (b, 0, 0)),
        pl.BlockSpec((None,) + new_kv.shape[1:], lambda b, p, pt: (b, 0, 0))]
    args = list(extra) + [q_bd, new_kv]
    if lf_new is not None:
        in_specs.append(pl.BlockSpec((None,) + lf_new.shape[1:], lambda b, p, pt: (b, 0, 0)))
        args.append(lf_new)
    in_specs += _page_specs(cache.shape[2:], layer, n_pages, reverse)
    args += [cache] * PAGES_PER_STEP
    if lf_cache is not None:
        in_specs += _page_specs((H_C, PAGE), layer, n_pages, reverse)
        args += [lf_cache] * PAGES_PER_STEP
    grid_spec = pltpu.PrefetchScalarGridSpec(
        num_scalar_prefetch=1, grid=(nb, steps), in_specs=in_specs,
        out_specs=pl.BlockSpec((None, out_rows, out_cols), lambda b, p, pt: (b, 0, 0)),
        scratch_shapes=scratch)
    return pl.pallas_call(
        body, grid_spec=grid_spec,
        out_shape=jax.ShapeDtypeStruct((nb, out_rows, out_cols), F32),
        compiler_params=_params("parallel", "arbitrary"), name=name)(page_table, *args)


def _mem_body(x_ref, wq_ref, wo_ref, g_ref, b_ref, kv_ref, o_ref, kvb_ref, oh_ref, *, d_m, tiled_rows, alpha):
    hd = H_M * d_m

    @pl.when(pl.program_id(1) == 0)
    def _():
        if not tiled_rows:
            kvb_ref[...] = kv_ref[...].astype(BF16)
            return
        n_mem = kvb_ref.shape[0]
        halves = d_m // LANES
        per_tok = 2 * halves * H_M
        for kv in range(2):
            for dt in range(halves):
                for h in range(H_M):
                    col = kv * hd + h * d_m + dt * LANES
                    row = (kv * halves + dt) * H_M + h
                    kvb_ref[:, col:col + LANES] = kv_ref[pl.ds(row, n_mem, stride=per_tok), :].astype(BF16)

    x = x_ref[...]
    q_all = _dot(x.astype(BF16), wq_ref[...]) * (d_m ** -0.5)
    for h in range(H_M):
        q = q_all[:, h * d_m:(h + 1) * d_m].astype(BF16)
        s = _dot_nt(q, kvb_ref[:, h * d_m:(h + 1) * d_m])
        p = jnp.exp(s - jnp.max(s, axis=1, keepdims=True))
        o = _dot(p.astype(BF16), kvb_ref[:, hd + h * d_m: hd + (h + 1) * d_m])
        oh_ref[:, h * d_m:(h + 1) * d_m] = (o / jnp.sum(p, axis=1, keepdims=True)).astype(BF16)
    y = _dot(oh_ref[...], wo_ref[...])
    o_ref[...] = _layer_norm(alpha * x + y, g_ref[...], b_ref[...])


def _mem_block(x, w_q, w_o, layer, mem_kv, kv_lead, g, b, *, nb, tq, n_mem, alpha, name):
    M, D = x.shape
    nq = M // nb // tq
    lead = tuple(kv_lead)
    row_tile = pl.BlockSpec((tq, D), lambda b, i: (b * nq + i, 0))
    weight = pl.BlockSpec((None, D, D), lambda b, i: (layer, 0, 0))
    vec = pl.BlockSpec((1, D), lambda b, i: (0, 0))
    return pl.pallas_call(
        functools.partial(_mem_body, d_m=D // H_M, tiled_rows=mem_kv.shape[-1] == LANES, alpha=alpha),
        grid=(nb, nq),
        in_specs=[row_tile, weight, weight, vec, vec,
                  pl.BlockSpec((None,) * (len(lead) + 1) + mem_kv.shape[-2:], lambda b, i: lead + (b, 0, 0))],
        out_specs=row_tile,
        out_shape=jax.ShapeDtypeStruct((M, D), F32),
        scratch_shapes=[pltpu.VMEM((n_mem, 2 * D), BF16), pltpu.VMEM((tq, D), BF16)],
        compiler_params=_params("parallel", "arbitrary"), name=name)(x, w_q, w_o, g, b, mem_kv)


def _mm_ln_body(*refs, n_in, alpha):
    xs = refs[:n_in]
    ws = refs[n_in:2 * n_in]
    r_ref, g_ref, b_ref, o_ref = refs[2 * n_in:]
    y = _dot(xs[0][...].astype(BF16), ws[0][...])
    for x_ref, w_ref in zip(xs[1:], ws[1:]):
        y = y + _dot(x_ref[...].astype(BF16), w_ref[...])
    o_ref[...] = _layer_norm(alpha * r_ref[...] + y, g_ref[...], b_ref[...])


def _mm_ln(xs, w, layer, res, g, b, *, tm, alpha, name):
    M, D = res.shape
    in_specs, args, off = [], [], 0
    for x in xs:
        in_specs.append(pl.BlockSpec((tm, x.shape[1]), lambda m: (m, 0)))
    for x in xs:
        kx = x.shape[1]
        in_specs.append(pl.BlockSpec((None, kx, D), lambda m, o=off // kx: (layer, o, 0)))
        off += kx
    in_specs += [pl.BlockSpec((tm, D), lambda m: (m, 0)),
                 pl.BlockSpec((1, D), lambda m: (0, 0)), pl.BlockSpec((1, D), lambda m: (0, 0))]
    return pl.pallas_call(
        functools.partial(_mm_ln_body, n_in=len(xs), alpha=alpha), grid=(M // tm,),
        in_specs=in_specs, out_specs=pl.BlockSpec((tm, D), lambda m: (m, 0)),
        out_shape=jax.ShapeDtypeStruct((M, D), F32),
        compiler_params=_params("parallel"), name=name)(*xs, *([w] * len(xs)), res, g, b)


def _swiglu_chunk(xb, wg_ref, wu_ref, wd_ref):
    hg = _dot(xb, wg_ref[...])
    hu = _dot(xb, wu_ref[...])
    a = hg * jax.nn.sigmoid(hg) * hu
    return _dot(a.astype(BF16), wd_ref[...])


def _ffn_body(x_ref, wg_ref, wu_ref, wd_ref, g_ref, b_ref, o_ref, xb_ref, acc_ref, *, alpha):
    f = pl.program_id(1)

    @pl.when(f == 0)
    def _():
        xb_ref[...] = x_ref[...].astype(BF16)
        acc_ref[...] = jnp.zeros_like(acc_ref)

    acc_ref[...] += _swiglu_chunk(xb_ref[...], wg_ref, wu_ref, wd_ref)

    @pl.when(f == pl.num_programs(1) - 1)
    def _():
        o_ref[...] = _layer_norm(alpha * x_ref[...] + acc_ref[...], g_ref[...], b_ref[...])


def _ffn(x, w_gu, w_down, layer, g, b, *, tm, tf, alpha, name):
    M, D = x.shape
    nf = w_down.shape[1] // tf
    return pl.pallas_call(
        functools.partial(_ffn_body, alpha=alpha), grid=(M // tm, nf),
        in_specs=[pl.BlockSpec((tm, D), lambda m, f: (m, 0)),
                  pl.BlockSpec((None, D, tf), lambda m, f: (layer, 0, f)),
                  pl.BlockSpec((None, D, tf), lambda m, f: (layer, 0, nf + f)),
                  pl.BlockSpec((None, tf, D), lambda m, f: (layer, f, 0)),
                  pl.BlockSpec((1, D), lambda m, f: (0, 0)), pl.BlockSpec((1, D), lambda m, f: (0, 0))],
        out_specs=pl.BlockSpec((tm, D), lambda m, f: (m, 0)),
        out_shape=jax.ShapeDtypeStruct((M, D), F32),
        scratch_shapes=[pltpu.VMEM((tm, D), BF16), pltpu.VMEM((tm, D), F32)],
        compiler_params=_params("parallel", "arbitrary"), name=name)(x, w_gu, w_gu, w_down, g, b)


def _moe_body(x_ref, wr_ref, wg_ref, wu_ref, wd_ref, g_ref, b_ref, o_ref,
              comb_ref, rank_ref, comb_t_ref, rank_t_ref, xc_ref, yc_ref, acc_ref, cnt_ref, *, alpha):
    e = pl.program_id(1)
    f = pl.program_id(2)
    last_f = f == pl.num_programs(2) - 1
    T = x_ref.shape[0]

    @pl.when((e == 0) & (f == 0))
    def _():
        x = x_ref[...]
        acc_ref[...] = jnp.zeros_like(acc_ref)
        logits = jnp.dot(x, wr_ref[...], precision=lax.Precision.HIGHEST, preferred_element_type=F32)
        lane = _iota(logits.shape, 1).astype(F32)
        logits = jnp.where(lane < N_EXPERTS, logits, NEG)
        m1 = jnp.max(logits, axis=1, keepdims=True)
        i1 = jnp.min(jnp.where(logits == m1, lane, float(LANES)), axis=1, keepdims=True)
        rest = jnp.where(lane == i1, NEG, logits)
        m2 = jnp.max(rest, axis=1, keepdims=True)
        i2 = jnp.min(jnp.where(rest == m2, lane, float(LANES)), axis=1, keepdims=True)
        e2 = jnp.exp(m2 - m1)
        den = 1.0 + e2
        comb = jnp.where(lane == i1, 1.0 / den, 0.0) + jnp.where(lane == i2, e2 / den, 0.0)
        comb_ref[...] = comb
        comb_t = comb.T
        comb_t_ref[...] = comb_t
        r = _iota((T, T), 0)
        c = _iota((T, T), 1)
        rank_ref[...] = _dot((c < r).astype(BF16), (comb > 0.0).astype(BF16))
        rank_t_ref[...] = _dot((comb_t > 0.0).astype(BF16), (r < c).astype(BF16))

    sub = _iota((LANES, T), 0)

    @pl.when(f == 0)
    def _():
        comb_t = comb_t_ref[...]
        routed_t = jnp.sum(jnp.where((sub == e) & (comb_t > 0.0), 1.0, 0.0), axis=0, keepdims=True)
        rank_t = jnp.sum(jnp.where(sub == e, rank_t_ref[...], 0.0), axis=0, keepdims=True)
        n_rows = jnp.sum(routed_t).astype(jnp.int32)
        cnt_ref[0] = (n_rows + MOE_CHUNK - 1) // MOE_CHUNK
        xb = x_ref[...].astype(BF16)
        slot = _iota((MOE_CHUNK, T), 0).astype(F32)

        def compact(ci, carry):
            base = pl.multiple_of(ci * MOE_CHUNK, MOE_CHUNK)
            pick = ((rank_t == slot + base.astype(F32)) & (routed_t > 0.0)).astype(BF16)
            xc_ref[pl.ds(base, MOE_CHUNK), :] = _dot(pick, xb).astype(BF16)
            yc_ref[pl.ds(base, MOE_CHUNK), :] = jnp.zeros((MOE_CHUNK, yc_ref.shape[1]), F32)
            return carry

        lax.fori_loop(0, cnt_ref[0], compact, 0)

    def expert(ci, carry):
        base = pl.multiple_of(ci * MOE_CHUNK, MOE_CHUNK)
        yc_ref[pl.ds(base, MOE_CHUNK), :] += _swiglu_chunk(xc_ref[pl.ds(base, MOE_CHUNK), :], wg_ref, wu_ref, wd_ref)
        return carry

    lax.fori_loop(0, cnt_ref[0], expert, 0)

    @pl.when(last_f)
    def _():
        comb = comb_ref[...]
        lane = _iota(comb.shape, 1)
        gate = jnp.sum(jnp.where(lane == e, comb, 0.0), axis=1, keepdims=True)
        rank = jnp.sum(jnp.where(lane == e, rank_ref[...], 0.0), axis=1, keepdims=True)
        slot = _iota((T, MOE_CHUNK), 1).astype(F32)

        def scatter(ci, carry):
            base = pl.multiple_of(ci * MOE_CHUNK, MOE_CHUNK)
            place = ((rank == slot + base.astype(F32)) & (gate > 0.0)).astype(BF16)
            acc_ref[...] += gate * _dot(place, yc_ref[pl.ds(base, MOE_CHUNK), :].astype(BF16))
            return carry

        lax.fori_loop(0, cnt_ref[0], scatter, 0)

    @pl.when(last_f & (e == pl.num_programs(1) - 1))
    def _():
        o_ref[...] = _layer_norm(alpha * x_ref[...] + acc_ref[...], g_ref[...], b_ref[...])


def _moe(x, w_router, w_gu, w_down, layer, g, b, *, tm, tf, alpha, name):
    M, D = x.shape
    nf = w_down.shape[2] // tf
    assert tm % MOE_CHUNK == 0 and M % tm == 0
    once = pl.Buffered(1)
    return pl.pallas_call(
        functools.partial(_moe_body, alpha=alpha), grid=(M // tm, N_EXPERTS, nf),
        in_specs=[pl.BlockSpec((tm, D), lambda m, e, f: (m, 0), pipeline_mode=once),
                  pl.BlockSpec((None, D, LANES), lambda m, e, f: (layer, 0, 0), pipeline_mode=once),
                  pl.BlockSpec((None, None, D, tf), lambda m, e, f: (layer, e, 0, f)),
                  pl.BlockSpec((None, None, D, tf), lambda m, e, f: (layer, e, 0, nf + f)),
                  pl.BlockSpec((None, None, tf, D), lambda m, e, f: (layer, e, f, 0)),
                  pl.BlockSpec((1, D), lambda m, e, f: (0, 0)), pl.BlockSpec((1, D), lambda m, e, f: (0, 0))],
        out_specs=pl.BlockSpec((tm, D), lambda m, e, f: (m, 0)),
        out_shape=jax.ShapeDtypeStruct((M, D), F32),
        scratch_shapes=[pltpu.VMEM((tm, LANES), F32), pltpu.VMEM((tm, LANES), F32),
                        pltpu.VMEM((LANES, tm), F32), pltpu.VMEM((LANES, tm), F32),
                        pltpu.VMEM((tm, D), BF16), pltpu.VMEM((tm, D), F32), pltpu.VMEM((tm, D), F32),
                        pltpu.SMEM((1,), jnp.int32)],
        compiler_params=_params("parallel", "arbitrary", "arbitrary"), name=name)(
            x, w_router, w_gu, w_gu, w_down, g, b)


def _rope_tables(pos):
    half = DK_A // 2
    inv = 1.0 / (ROPE_THETA ** (jnp.arange(half, dtype=F32) / half))
    ang = pos.astype(F32)[:, None] * inv[None, :]
    cos, sin = jnp.cos(ang), jnp.sin(ang)
    return jnp.concatenate([cos] * 4, axis=1), jnp.concatenate([-sin, sin, -sin, sin], axis=1)


def _block_diag_queries(q, nb, n_heads, n_maps, d, rows_per_head, scale):
    g = n_heads * n_maps
    lq = q.shape[0] // nb
    q5 = (q * scale).reshape(nb, lq, g, d)
    q5 = jnp.pad(q5, ((0, 0), (0, rows_per_head - lq), (0, 0), (0, 0)))
    eye = jnp.eye(g, dtype=q.dtype)
    out = jnp.einsum('bigd,gx->bgixd', q5, eye)
    return out.reshape(nb, g * rows_per_head, g * d).astype(BF16)


def _pad_rows(x, nb, rows):
    lq = x.shape[0] // nb
    return jnp.pad(x.reshape(nb, lq, x.shape[1]), ((0, 0), (0, rows - lq), (0, 0)))


def kernel(x_prompt, x_sample, cache_a_kv, cache_b_kv, cache_c_kv, cache_c_logf, cache_mem_kv,
           page_table, mem_prompt, even_w_in, even_w_out, diff_lambda, diff_subln_g,
           odd_w_in, odd_b_f, odd_w_out, mem_w_q, mem_w_kv, mem_w_o, ffn_w_gu, ffn_w_down,
           moe_w_router, moe_w_gu, moe_w_down, ln_g, ln_b):
    depth = ln_g.shape[0]
    alpha = (2 * depth) ** 0.25
    nbp, L, D = x_prompt.shape
    nbs, Ls, _ = x_sample.shape
    n_pages = page_table.shape[1]
    n_pool = cache_a_kv.shape[1]
    past_len = n_pages * PAGE
    n_mem = mem_prompt.shape[1]
    assert Ls == 8 and n_pages % PAGES_PER_STEP == 0
    qa_w = H_A * 2 * DK_A
    qb_w = H_B * D_B
    qc_w = H_C * D_C
    even_in = 3 * qa_w + 3 * qb_w
    odd_main = 3 * qc_w

    w_even_in = even_w_in.astype(BF16)
    w_even_out = even_w_out.astype(BF16)
    w_odd_in = odd_w_in[:, :, :odd_main].astype(BF16)
    w_odd_f = jnp.pad(odd_w_in[:, :, odd_main:], ((0, 0), (0, 0), (0, LANES - H_C))).astype(BF16)
    b_odd_f = jnp.pad(odd_b_f, ((0, 0), (0, LANES - H_C)))[:, None, :]
    w_odd_out = odd_w_out.astype(BF16)
    w_mem_q = mem_w_q.astype(BF16)
    w_mem_kv = mem_w_kv.astype(BF16)
    w_mem_o = mem_w_o.astype(BF16)
    w_ffn_gu = ffn_w_gu.astype(BF16)
    w_ffn_down = ffn_w_down.astype(BF16)
    w_moe_gu = moe_w_gu.astype(BF16)
    w_moe_down = moe_w_down.astype(BF16)
    w_router = jnp.pad(moe_w_router, ((0, 0), (0, 0), (0, LANES - N_EXPERTS)))

    ca = cache_a_kv.reshape(cache_a_kv.shape[0], n_pool, PAGE * 2 * H_A, 2 * DK_A)
    cb = jnp.transpose(cache_b_kv, (0, 1, 3, 4, 5, 2)).reshape(cache_b_kv.shape[0], n_pool, 2, qb_w, PAGE)
    cc = jnp.transpose(cache_c_kv, (0, 1, 3, 4, 5, 2)).reshape(cache_c_kv.shape[0], n_pool, 2, qc_w, PAGE)
    clf = jnp.swapaxes(cache_c_logf, 2, 3)
    d_m = D // H_M
    cmem = cache_mem_kv.reshape(depth, nbs, n_mem, 2, H_M, d_m // LANES, LANES)
    cmem = jnp.swapaxes(cmem, 4, 5).reshape(depth, nbs, n_mem * 2 * d_m // LANES * H_M, LANES)

    rope_p = _rope_tables(jnp.arange(L))
    rope_s = _rope_tables(past_len + (jnp.arange(nbs * Ls) % Ls))

    xp = x_prompt.reshape(nbp * L, D)
    xs = x_sample.reshape(nbs * Ls, D)
    mp = mem_prompt.reshape(nbp * n_mem, D)
    Mp, Ms = xp.shape[0], xs.shape[0]
    tmp, tms = min(1024, Mp), Ms
    tq = min(512, L)
    tk = 2 * tq if (L // tq) % 2 == 0 else tq
    tq_sb = min(256, L)

    a_p, a_s, b_p, b_s, c_p, c_s, lf_p, lf_s, mem_rows = [], [], [], [], [], [], [], [], []
    for l in range(depth):
        i = l // 2
        g3 = ln_g[l][:, None, :]
        b3 = ln_b[l][:, None, :]
        mem_kv_p = _proj(mp, w_mem_kv, l, 2 * D, tm=min(1024, mp.shape[0]), name=f"memkv{l}")
        mem_rows.append(mem_kv_p.reshape(nbp, n_mem, 2, H_M, D // H_M))
        mem_kv_p = mem_kv_p.reshape(nbp, n_mem, 2 * D)
        if l % 2 == 0:
            lam_init = _lambda_init(l)
            lv = diff_lambda[i]
            sg = diff_subln_g[i][None, :]
            hp = _proj(xp, w_even_in, i, even_in, tm=tmp, rope=rope_p, rope_cols=2 * qa_w, name=f"even_in_p{l}")
            a_p.append(hp[:, qa_w:3 * qa_w].reshape(nbp, L, 2, H_A, 2 * DK_A))
            b_p.append(hp[:, 3 * qa_w + qb_w:].reshape(nbp, L, 2, H_B, D_B))
            oa = _prompt_attn(functools.partial(_diff_body, tq=tq, tk=tk, lam_init=lam_init), hp, nb=nbp, L=L,
                              n_groups=H_A, qcol=0, kcol=H_A, vcol=2 * H_A, tq=tq, scratch=(), extra=(lv, sg),
                              extra_specs=(pl.BlockSpec(lv.shape, lambda b, g, i: (0, 0)),
                                           pl.BlockSpec(sg.shape, lambda b, g, i: (0, 0))),
                              name=f"diff_p{l}")
            ob = _prompt_attn(functools.partial(_sb_body, tq=tq_sb), hp, nb=nbp, L=L, n_groups=H_B // 2,
                              qcol=3 * H_A, kcol=3 * H_A + H_B // 2, vcol=3 * H_A + H_B, tq=tq_sb, scratch=(),
                              name=f"stick_p{l}")
            xp = _mm_ln([oa, ob], w_even_out, i, xp, g3[0], b3[0], tm=tmp, alpha=alpha, name=f"even_out_p{l}")
            hs = _proj(xs, w_even_in, i, even_in, tm=tms, rope=rope_s, rope_cols=2 * qa_w, name=f"even_in_s{l}")
            a_new = hs[:, qa_w:3 * qa_w]
            b_new = hs[:, 3 * qa_w + qb_w:]
            a_s.append(a_new.reshape(nbs, Ls, 2, H_A, 2 * DK_A))
            b_s.append(b_new.reshape(nbs, Ls, 2, H_B, D_B))
            qa_bd = _block_diag_queries(hs[:, :qa_w], nbs, H_A, 2, DK_A, 16, DK_A ** -0.5)
            qb_bd = _block_diag_queries(hs[:, 3 * qa_w:3 * qa_w + qb_w], nbs, H_B, 1, D_B, 16, D_B ** -0.5)
            oa = _decode_attn(functools.partial(_dec_diff_body, lam_init=lam_init), page_table, qa_bd,
                              _pad_rows(a_new, nbs, PAGE), ca, i, out_rows=16, out_cols=qa_w,
                              scratch=[pltpu.VMEM((LANES, 1), F32), pltpu.VMEM((LANES, 1), F32),
                                       pltpu.VMEM((LANES, qa_w), F32)],
                              reverse=False, extra=(lv, sg),
                              extra_specs=(pl.BlockSpec(lv.shape, lambda b, p, pt: (0, 0)),
                                           pl.BlockSpec(sg.shape, lambda b, p, pt: (0, 0))),
                              name=f"diff_s{l}")
            ob = _decode_stick(page_table, qb_bd, _pad_rows(b_new, nbs, PAGE), cb, i, name=f"stick_s{l}")
            oa = oa[:, :Ls].reshape(Ms, qa_w)
            ob = ob[:, :Ls].reshape(Ms, qb_w)
            xs = _mm_ln([oa, ob], w_even_out, i, xs, g3[0], b3[0], tm=tms, alpha=alpha, name=f"even_out_s{l}")
        else:
            hp = _proj(xp, w_odd_in, i, odd_main, tm=tmp, name=f"odd_in_p{l}")
            c_p.append(hp[:, qc_w:].reshape(nbp, L, 2, H_C, D_C))
            lf, cs = _logf(xp, w_odd_f[i], b_odd_f[i], nb=nbp, tl=tq_sb, with_cumsum=True, name=f"logf_p{l}")
            lf_p.append(lf[:, :H_C].reshape(nbp, L, H_C))
            nq = L // tq
            oc = _prompt_attn(functools.partial(_fox_body, tq=tq, tk=tk), hp, nb=nbp, L=L, n_groups=H_C // 2,
                              qcol=0, kcol=H_C // 2, vcol=H_C, tq=tq, scratch=(2,), extra=(cs, cs),
                              extra_specs=(pl.BlockSpec((tq, LANES), lambda b, g, i: (b * nq + i, 0)),
                                           pl.BlockSpec((L, LANES), lambda b, g, i: (b, 0))),
                              name=f"fox_p{l}")
            xp = _mm_ln([oc], w_odd_out, i, xp, g3[0], b3[0], tm=tmp, alpha=alpha, name=f"odd_out_p{l}")
            hs = _proj(xs, w_odd_in, i, odd_main, tm=tms, name=f"odd_in_s{l}")
            c_new = hs[:, qc_w:]
            c_s.append(c_new.reshape(nbs, Ls, 2, H_C, D_C))
            lf, _ = _logf(xs, w_odd_f[i], b_odd_f[i], nb=1, tl=Ms, with_cumsum=False, name=f"logf_s{l}")
            lf_s.append(lf[:, :H_C].reshape(nbs, Ls, H_C))
            lf_new_t = jnp.swapaxes(_pad_rows(lf[:, :H_C], nbs, PAGE), 1, 2)
            qc_bd = _block_diag_queries(hs[:, :qc_w], nbs, H_C, 1, D_C, 8, D_C ** -0.5)
            oc = _decode_attn(_dec_fox_body, page_table, qc_bd, _pad_rows(c_new, nbs, PAGE), cc, i,
                              out_rows=8, out_cols=qc_w,
                              scratch=[pltpu.VMEM((LANES, 1), F32), pltpu.VMEM((LANES, 1), F32),
                                       pltpu.VMEM((LANES, qc_w), F32), pltpu.VMEM((H_C, 1), F32)],
                              reverse=True, lf_new=lf_new_t, lf_cache=clf, name=f"fox_s{l}")
            oc = oc.reshape(Ms, qc_w)
            xs = _mm_ln([oc], w_odd_out, i, xs, g3[0], b3[0], tm=tms, alpha=alpha, name=f"odd_out_s{l}")

        xp = _mem_block(xp, w_mem_q, w_mem_o, l, mem_kv_p, (), g3[1], b3[1], nb=nbp, tq=tq, n_mem=n_mem,
                        alpha=alpha, name=f"mem_p{l}")
        xs = _mem_block(xs, w_mem_q, w_mem_o, l, cmem, (l,), g3[1], b3[1], nb=nbs, tq=Ls, n_mem=n_mem,
                        alpha=alpha, name=f"mem_s{l}")

        if l % 2 == 0:
            xp = _ffn(xp, w_ffn_gu, w_ffn_down, i, g3[2], b3[2], tm=min(1024, Mp), tf=256, alpha=alpha, name=f"ffn_p{l}")
            xs = _ffn(xs, w_ffn_gu, w_ffn_down, i, g3[2], b3[2], tm=tms, tf=256, alpha=alpha, name=f"ffn_s{l}")
        else:
            tf_moe = w_moe_down.shape[2] // 2
            xp = _moe(xp, w_router, w_moe_gu, w_moe_down, i, g3[2], b3[2], tm=min(1024, Mp), tf=tf_moe,
                      alpha=alpha, name=f"moe_p{l}")
            xs = _moe(xs, w_router, w_moe_gu, w_moe_down, i, g3[2], b3[2], tm=tms, tf=tf_moe, alpha=alpha,
                      name=f"moe_s{l}")

    return (xp.reshape(nbp, L, D), xs.reshape(nbs, Ls, D),
            jnp.stack(a_p), jnp.stack(a_s), jnp.stack(b_p), jnp.stack(b_s),
            jnp.stack(c_p), jnp.stack(c_s), jnp.stack(lf_p), jnp.stack(lf_s), jnp.stack(mem_rows))
```
